```python
import math
import jax, jax.numpy as jnp
from jax import lax
import numpy as np

D_MODEL = 1024
BATCH = 8
SEQ = 8192
DEPTH = 2
DEC_BATCH = 16
DEC_SEQ = 32
PAST_LEN = 4096

CHUNK = 64
Q_BLOCK = 128
ROPE_THETA = 500000.0
N_A = (DEPTH + 1) // 2
N_C = DEPTH // 2
H_MLA = 8
MLA_NOPE = 64
MLA_ROPE = 32
MLA_V = 64
Q_LORA = 512
KV_LORA = 256
H_FOX = 8
FOX_DH = 64
FORGET_BIAS_LO = 2.0
FORGET_BIAS_HI = 6.0
H_DIFF = 8
DIFF_DH = 64
DIFF_ROT = DIFF_DH // 4
C_W = H_DIFF * 2 * DIFF_DH
D_FF = 4 * D_MODEL
FOX_W = H_FOX * FOX_DH
A_IN = Q_LORA + KV_LORA + MLA_ROPE + 3 * FOX_W + H_FOX
A_OUT = H_MLA * MLA_V + FOX_W

kernel_name = 'chunk_causal_mla_fox_diffattn_step'


def _rms(x, g, eps=1e-6):
    xf = x.astype(jnp.float32)
    y = xf * lax.rsqrt(jnp.mean(xf * xf, axis=-1, keepdims=True) + eps)
    return (y * g.astype(jnp.float32)).astype(x.dtype)


def _rope(x, pos, rot):
    half = rot // 2
    inv = ROPE_THETA ** (-jnp.arange(half, dtype=jnp.float32) / half)
    ang = pos.astype(jnp.float32)[:, None] * inv[None, :]
    cos = jnp.cos(ang)[None, :, None, :]
    sin = jnp.sin(ang)[None, :, None, :]
    xr = x[..., :rot].astype(jnp.float32)
    x1, x2 = xr[..., :half], xr[..., half:]
    r = jnp.concatenate([x1 * cos - x2 * sin, x2 * cos + x1 * sin], axis=-1).astype(x.dtype)
    return jnp.concatenate([r, x[..., rot:]], axis=-1)


def _mask(q_pos, k_pos, frame_causal):
    if frame_causal:
        return k_pos[None, :] <= q_pos[:, None]
    return (k_pos[None, :] // CHUNK) <= (q_pos[:, None] // CHUNK)


def _probs(q, k, q_pos, k_pos, frame_causal, q_cum=None, k_cum=None):
    s = jnp.einsum('bqhd,bkhd->bhqk', q, k, preferred_element_type=jnp.float32) * (q.shape[-1] ** -0.5)
    if q_cum is not None:
        s = s + jnp.swapaxes(q_cum, 1, 2)[..., :, None] - jnp.swapaxes(k_cum, 1, 2)[..., None, :]
    s = jnp.where(_mask(q_pos, k_pos, frame_causal)[None, None], s, -jnp.inf)
    return jax.nn.softmax(s, axis=-1)


def _apply(p, v):
    return jnp.einsum('bhqk,bkhe->bqhe', p, v.astype(jnp.float32)).astype(v.dtype)


def _sweep(fn, q_args, q_pos):
    nb = q_pos.shape[0] // Q_BLOCK

    def split(a):
        return jnp.moveaxis(a.reshape(a.shape[0], nb, Q_BLOCK, *a.shape[2:]), 1, 0)

    xs = tuple(split(a) for a in q_args) + (q_pos.reshape(nb, Q_BLOCK),)
    out = lax.map(lambda blk: fn(*blk), xs)
    out = jnp.moveaxis(out, 0, 1)
    return out.reshape(out.shape[0], nb * Q_BLOCK, *out.shape[3:])


def _a_project(h, pos, w_in, g_q, w_uq, g_kv, b_f):
    B, T, _ = h.shape
    proj = h @ w_in
    o1 = Q_LORA
    o2 = o1 + KV_LORA
    o3 = o2 + MLA_ROPE
    o4 = o3 + FOX_W
    o5 = o4 + FOX_W
    o6 = o5 + FOX_W
    q = (_rms(proj[..., :o1], g_q) @ w_uq).reshape(B, T, H_MLA, MLA_NOPE + MLA_ROPE)
    q = jnp.concatenate([q[..., :MLA_NOPE], _rope(q[..., MLA_NOPE:], pos, MLA_ROPE)], axis=-1)
    ckv = _rms(proj[..., o1:o2], g_kv)
    krope = _rope(proj[..., o2:o3][:, :, None, :], pos, MLA_ROPE)[:, :, 0, :]
    fq = proj[..., o3:o4].reshape(B, T, H_FOX, FOX_DH)
    fk = proj[..., o4:o5].reshape(B, T, H_FOX, FOX_DH)
    fv = proj[..., o5:o6].reshape(B, T, H_FOX, FOX_DH)
    logf = jax.nn.log_sigmoid((proj[..., o6:] + b_f).astype(jnp.float32))
    return q, fq, (ckv, krope, fk, fv, logf)


def _a_attend(qm, fq, cum_q, q_pos, km, vm, fk, fv, cum_k, k_pos):
    B, Tq = qm.shape[:2]
    om = _apply(_probs(qm, km, q_pos, k_pos, False), vm)
    of = _apply(_probs(fq, fk, q_pos, k_pos, True, cum_q, cum_k), fv)
    return jnp.concatenate([om.reshape(B, Tq, -1), of.reshape(B, Tq, -1)], axis=-1)


def _mixer_a(h, q_pos, past, w_in, g_q, w_uq, g_kv, w_ukv, b_f, w_out):
    T = h.shape[1]
    qm, fq, new = _a_project(h, q_pos, w_in, g_q, w_uq, g_kv, b_f)
    rows = new if past is None else tuple(
        jnp.concatenate([p.astype(n.dtype), n], axis=1) for p, n in zip(past, new))
    ckv, krope, fk, fv, logf = rows
    B, L = ckv.shape[:2]
    k_pos = jnp.arange(L)
    kv = (ckv @ w_ukv).reshape(B, L, H_MLA, MLA_NOPE + MLA_V)
    km = jnp.concatenate(
        [kv[..., :MLA_NOPE], jnp.broadcast_to(krope[:, :, None, :], (B, L, H_MLA, MLA_ROPE))], axis=-1)
    vm = kv[..., MLA_NOPE:]
    cum = jnp.cumsum(logf.astype(jnp.float32), axis=1)
    cum_q = cum[:, L - T:]
    attend = lambda a, b, c, p: _a_attend(a, b, c, p, km, vm, fk, fv, cum, k_pos)
    if past is None:
        o = _sweep(attend, (qm, fq, cum_q), q_pos)
    else:
        o = attend(qm, fq, cum_q, q_pos)
    return o @ w_out, new


def _c_project(h, pos, w_in):
    B, T, _ = h.shape
    proj = h @ w_in

    def heads(a):
        a = a.reshape(B, T, 2 * H_DIFF, DIFF_DH)
        return _rope(a, pos, DIFF_ROT).reshape(B, T, H_DIFF, 2 * DIFF_DH)

    q = heads(proj[..., :C_W])
    k = heads(proj[..., C_W:2 * C_W])
    v = proj[..., 2 * C_W:].reshape(B, T, H_DIFF, 2 * DIFF_DH)
    return q, k, v


def _c_attend(q, q_pos, k, v, k_pos, lam, g_sub, lam_init):
    B, Tq = q.shape[:2]
    p1 = _probs(q[..., :DIFF_DH], k[..., :DIFF_DH], q_pos, k_pos, False)
    p2 = _probs(q[..., DIFF_DH:], k[..., DIFF_DH:], q_pos, k_pos, False)
    o = jnp.einsum('bhqk,bkhe->bqhe', p1 - lam * p2, v.astype(jnp.float32))
    o = _rms(o, g_sub, 1e-5) * (1.0 - lam_init)
    return o.reshape(B, Tq, C_W).astype(v.dtype)


def _mixer_c(h, q_pos, past, w_in, lam_p, g_sub, w_out, lam_init):
    q, k, v = _c_project(h, q_pos, w_in)
    new = (k, v)
    rows = new if past is None else tuple(
        jnp.concatenate([p.astype(n.dtype), n], axis=1) for p, n in zip(past, new))
    ka, va = rows
    k_pos = jnp.arange(ka.shape[1])
    lp = lam_p.astype(jnp.float32)
    lam = jnp.exp(jnp.sum(lp[0] * lp[1])) - jnp.exp(jnp.sum(lp[2] * lp[3])) + lam_init
    attend = lambda a, p: _c_attend(a, p, ka, va, k_pos, lam, g_sub, lam_init)
    if past is None:
        o = _sweep(attend, (q,), q_pos)
    else:
        o = attend(q, q_pos)
    return o @ w_out, new


def _mlp(h, w_up, w_down):
    return jnp.square(jax.nn.relu(h @ w_up)) @ w_down


def _stack(rows, j):
    return jnp.stack([r[j] for r in rows])


def setup_inputs(seed: int = 0) -> dict:
    key = jax.random.key(seed)
    ks = iter(jax.random.split(key, 40))

    def nrm(shape, scale=1.0):
        return jax.random.normal(next(ks), shape, jnp.float32) * scale

    def gain(shape):
        return 1.0 + nrm(shape, 0.01)

    forget_bias = jnp.linspace(FORGET_BIAS_LO, FORGET_BIAS_HI, H_FOX, dtype=jnp.float32)
    return {
        'x_prompt': nrm((BATCH, SEQ, D_MODEL)),
        'x_sample': nrm((DEC_BATCH, DEC_SEQ, D_MODEL)),
        'cache_mla_ckv': nrm((N_A, DEC_BATCH, PAST_LEN, KV_LORA)),
        'cache_mla_krope': nrm((N_A, DEC_BATCH, PAST_LEN, MLA_ROPE)),
        'cache_fox_k': nrm((N_A, DEC_BATCH, PAST_LEN, H_FOX, FOX_DH)),
        'cache_fox_v': nrm((N_A, DEC_BATCH, PAST_LEN, H_FOX, FOX_DH)),
        'cache_fox_logf': jax.nn.log_sigmoid(nrm((N_A, DEC_BATCH, PAST_LEN, H_FOX)) + forget_bias),
        'cache_diff_k': nrm((N_C, DEC_BATCH, PAST_LEN, H_DIFF, 2 * DIFF_DH)),
        'cache_diff_v': nrm((N_C, DEC_BATCH, PAST_LEN, H_DIFF, 2 * DIFF_DH)),
        'g_mix': gain((DEPTH, D_MODEL)),
        'a_w_in': nrm((N_A, D_MODEL, A_IN), D_MODEL ** -0.5),
        'a_g_q': gain((N_A, Q_LORA)),
        'a_w_uq': nrm((N_A, Q_LORA, H_MLA * (MLA_NOPE + MLA_ROPE)), Q_LORA ** -0.5),
        'a_g_kv': gain((N_A, KV_LORA)),
        'a_w_ukv': nrm((N_A, KV_LORA, H_MLA * (MLA_NOPE + MLA_V)), KV_LORA ** -0.5),
        'a_b_f': forget_bias[None, :] + nrm((N_A, H_FOX), 0.1),
        'a_w_out': nrm((N_A, A_OUT, D_MODEL), A_OUT ** -0.5),
        'c_w_in': nrm((N_C, D_MODEL, 3 * C_W), D_MODEL ** -0.5),
        'c_lam': nrm((N_C, 4, DIFF_DH), 0.1),
        'c_g_sub': gain((N_C, 2 * DIFF_DH)),
        'c_w_out': nrm((N_C, C_W, D_MODEL), C_W ** -0.5),
        'g_mlp': gain((DEPTH, D_MODEL)),
        'w_up': nrm((DEPTH, D_MODEL, D_FF), D_MODEL ** -0.5),
        'w_down': nrm((DEPTH, D_FF, D_MODEL), D_FF ** -0.5),
        'g_final': gain((D_MODEL,)),
    }


def reference(x_prompt, x_sample, cache_mla_ckv, cache_mla_krope, cache_fox_k, cache_fox_v,
              cache_fox_logf, cache_diff_k, cache_diff_v, g_mix, a_w_in, a_g_q, a_w_uq, a_g_kv,
              a_w_ukv, a_b_f, a_w_out, c_w_in, c_lam, c_g_sub, c_w_out, g_mlp, w_up, w_down,
              g_final):
    past_len = cache_mla_ckv.shape[2]
    pos_p = jnp.arange(x_prompt.shape[1])
    pos_s = past_len + jnp.arange(x_sample.shape[1])
    xp, xs = x_prompt, x_sample
    a_p, a_s, c_p, c_s = [], [], [], []
    for layer in range(DEPTH):
        i = layer // 2
        hp = _rms(xp, g_mix[layer])
        hs = _rms(xs, g_mix[layer])
        if layer % 2 == 0:
            wa = (a_w_in[i], a_g_q[i], a_w_uq[i], a_g_kv[i], a_w_ukv[i], a_b_f[i], a_w_out[i])
            past = (cache_mla_ckv[i], cache_mla_krope[i], cache_fox_k[i], cache_fox_v[i], cache_fox_logf[i])
            mp, rp = _mixer_a(hp, pos_p, None, *wa)
            ms, rs = _mixer_a(hs, pos_s, past, *wa)
            a_p.append(rp)
            a_s.append(rs)
        else:
            lam_init = 0.8 - 0.6 * math.exp(-0.3 * layer)
            wc = (c_w_in[i], c_lam[i], c_g_sub[i], c_w_out[i], lam_init)
            past = (cache_diff_k[i], cache_diff_v[i])
            mp, rp = _mixer_c(hp, pos_p, None, *wc)
            ms, rs = _mixer_c(hs, pos_s, past, *wc)
            c_p.append(rp)
            c_s.append(rs)
        xp = xp + mp
        xs = xs + ms
        xp = xp + _mlp(_rms(xp, g_mlp[layer]), w_up[layer], w_down[layer])
        xs = xs + _mlp(_rms(xs, g_mlp[layer]), w_up[layer], w_down[layer])
    y_prompt = _rms(xp, g_final)
    y_sample = _rms(xs, g_final)
    return (y_prompt, y_sample,
            _stack(a_p, 0), _stack(a_p, 1), _stack(a_p, 2), _stack(a_p, 3), _stack(a_p, 4),
            _stack(c_p, 0), _stack(c_p, 1),
            _stack(a_s, 0), _stack(a_s, 1), _stack(a_s, 2), _stack(a_s, 3), _stack(a_s, 4),
            _stack(c_s, 0), _stack(c_s, 1))
```

```python
import functools
import math

import jax
import jax.numpy as jnp
from jax import lax
from jax.experimental import pallas as pl
from jax.experimental.pallas import tpu as pltpu

F32 = jnp.float32
BF16 = jnp.bfloat16

CHUNK = 64
ROPE_THETA = 500000.0
H_MLA, MLA_NOPE, MLA_ROPE, MLA_V = 8, 64, 32, 64
Q_LORA, KV_LORA = 512, 256
H_FOX, FOX_DH = 8, 64
H_DIFF, DIFF_DH = 8, 64
DIFF_ROT = DIFF_DH // 4
LANE = 128
LOG2E = math.log2(math.e)
NEG = -1e30
VMEM_LIMIT = 56 * 1024 * 1024


def _cparams(*sem):
    return pltpu.CompilerParams(dimension_semantics=sem, vmem_limit_bytes=VMEM_LIMIT)


def _const_spec(shape):
    nd = len(shape)
    return pl.BlockSpec(shape, lambda *_: (0,) * nd, pipeline_mode=pl.Buffered(1))


def _rms(x, g, eps):
    y = x * lax.rsqrt(jnp.mean(x * x, axis=-1, keepdims=True) + eps)
    return y * g


def _dot(a, b):
    return jnp.dot(a, b, preferred_element_type=F32)


def _rope_slab(x, c, s1, s2, half):
    return x * c + pltpu.roll(x, half, 1) * s1 + pltpu.roll(x, LANE - half, 1) * s2


def _rope_wide(x, tab, base, half):
    c = tab[:, base:base + LANE]
    s1 = tab[:, base + LANE:base + 2 * LANE]
    s2 = tab[:, base + 2 * LANE:base + 3 * LANE]
    n = x.shape[1] // LANE
    return jnp.concatenate(
        [_rope_slab(x[:, j * LANE:(j + 1) * LANE], c, s1, s2, half) for j in range(n)], axis=1)


def _split3(x):
    hi = x.astype(BF16)
    r1 = x - hi.astype(F32)
    mid = r1.astype(BF16)
    lo = (r1 - mid.astype(F32)).astype(BF16)
    return hi, mid, lo


_O_Q, _O_CKV, _O_FQ, _O_FK, _O_FV, _O_KR, _O_F, _A_COLS = 0, 512, 768, 1280, 1792, 2304, 2432, 2560


def _proj_a_kernel(x_ref, tab_ref, gmix_ref, win_ref, gq_ref, wuq_ref, gkv_ref, wkv_ref, bf_ref,
                   ckv_o, krope_o, fk_o, fv_o, logf_o, lfs_o, qm_o, km_o, vm_o, fq_o, fkb_o, fvb_o,
                   *, fox_scale):
    h = _rms(x_ref[...], gmix_ref[...], 1e-6).astype(BF16)
    tab = tab_ref[...]
    ql = _dot(h, win_ref[:, _O_Q:_O_Q + Q_LORA])
    qn = _rms(ql, gq_ref[...], 1e-6).astype(BF16)
    q = _dot(qn, wuq_ref[...])
    qm_o[...] = _rope_wide(q, tab, 0, MLA_ROPE // 2).astype(BF16)
    cn = _rms(_dot(h, win_ref[:, _O_CKV:_O_CKV + KV_LORA]), gkv_ref[...], 1e-6)
    ckv_o[...] = cn
    kr = _rope_wide(_dot(h, win_ref[:, _O_KR:_O_KR + LANE]), tab, 3 * LANE, MLA_ROPE // 2)
    krope_o[...] = kr[:, :MLA_ROPE]
    kin = jnp.concatenate([cn.astype(BF16), kr.astype(BF16)], axis=1)
    kv = _dot(kin, wkv_ref[...])
    km_o[...] = kv[:, :H_MLA * LANE].astype(BF16)
    vm_o[...] = kv[:, H_MLA * LANE:].astype(BF16)
    fq_o[...] = (_dot(h, win_ref[:, _O_FQ:_O_FQ + 512]) * fox_scale).astype(BF16)
    fk = _dot(h, win_ref[:, _O_FK:_O_FK + 512])
    fk_o[...] = fk
    fkb_o[...] = fk.astype(BF16)
    fv = _dot(h, win_ref[:, _O_FV:_O_FV + 512])
    fv_o[...] = fv
    fvb_o[...] = fv.astype(BF16)
    z = _dot(h, win_ref[:, _O_F:_O_F + LANE]) + bf_ref[...]
    ls = jnp.minimum(z, 0.0) - jnp.log1p(jnp.exp(-jnp.abs(z)))
    lane = lax.broadcasted_iota(jnp.int32, ls.shape, 1)
    ls = jnp.where(lane < H_FOX, ls, 0.0)
    lfs_o[...] = ls
    logf_o[...] = ls[:, :H_FOX]


def _proj_a(x, tab, tab_period, gmix, win_p, gq, wuq_p, gkv, wkv_p, bf_slab, tm):
    n, d = x.shape
    nt = n // tm
    nper = tab_period // tm
    row = lambda w: pl.BlockSpec((tm, w), lambda i: (i, 0))
    outs = [(KV_LORA, F32), (MLA_ROPE, F32), (512, F32), (512, F32), (H_FOX, F32), (LANE, F32),
            (H_MLA * LANE, BF16), (H_MLA * LANE, BF16), (H_MLA * MLA_V, BF16),
            (512, BF16), (512, BF16), (512, BF16)]
    return pl.pallas_call(
        functools.partial(_proj_a_kernel, fox_scale=FOX_DH ** -0.5 * LOG2E),
        grid=(nt,),
        in_specs=[row(d),
                  pl.BlockSpec((tm, 6 * LANE), lambda i: (i % nper, 0)),
                  _const_spec(gmix.shape), _const_spec(win_p.shape), _const_spec(gq.shape),
                  _const_spec(wuq_p.shape), _const_spec(gkv.shape), _const_spec(wkv_p.shape),
                  _const_spec(bf_slab.shape)],
        out_specs=[row(w) for w, _ in outs],
        out_shape=[jax.ShapeDtypeStruct((n, w), t) for w, t in outs],
        compiler_params=_cparams("parallel"),
        name="proj_a",
    )(x, tab, gmix, win_p, gq, wuq_p, gkv, wkv_p, bf_slab)


def _cum_kernel(lf_ref, init_ref, tri_ref, ek_ref, eq_ref, onek_ref, oneq_ref,
                cum_o, kaug_o, qaug_o, carry):
    @pl.when(pl.program_id(1) == 0)
    def _():
        carry[...] = init_ref[0]

    tri = tri_ref[...]
    hi, mid, lo = _split3(lf_ref[0])
    cum = carry[...] + (_dot(tri, hi) + _dot(tri, mid) + _dot(tri, lo))
    t = cum.shape[0]
    carry[...] = cum[t - 1:t, :]
    cum_o[0] = cum
    hi, mid, lo = _split3(cum * LOG2E)
    kaug_o[...] = (_dot(hi, ek_ref[0]) + _dot(mid, ek_ref[1]) + _dot(lo, ek_ref[2])
                   + onek_ref[...]).astype(BF16)
    qaug_o[...] = (_dot(hi, eq_ref[0]) + _dot(mid, eq_ref[1]) + _dot(lo, eq_ref[2])
                   + oneq_ref[...]).astype(BF16)


def _aug_tables():
    w = (H_FOX // 2) * LANE
    ek = [[[0.0] * w for _ in range(LANE)] for _ in range(3)]
    eq = [[[0.0] * w for _ in range(LANE)] for _ in range(3)]
    onek = [0.0] * w
    oneq = [0.0] * w
    for hh in range(H_FOX):
        base = (hh // 2) * LANE + (hh % 2) * (LANE // 2)
        for s in range(3):
            ek[s][hh][base + s] = -1.0
            eq[s][hh][base + 3 + s] = 1.0
            onek[base + 3 + s] = 1.0
            oneq[base + s] = 1.0
    return (jnp.array(ek, BF16), jnp.array(eq, BF16),
            jnp.array([onek], F32), jnp.array([oneq], F32))


def _cum_aug(lf_slab, init, t):
    b, l, _ = lf_slab.shape
    nt = l // t
    tri = (lax.broadcasted_iota(jnp.int32, (t, t), 1)
           <= lax.broadcasted_iota(jnp.int32, (t, t), 0)).astype(BF16)
    ek, eq, onek, oneq = _aug_tables()
    w = (H_FOX // 2) * LANE
    return pl.pallas_call(
        _cum_kernel,
        grid=(b, nt),
        in_specs=[pl.BlockSpec((1, t, LANE), lambda i, j: (i, j, 0)),
                  pl.BlockSpec((1, 1, LANE), lambda i, j: (i, 0, 0)),
                  _const_spec(tri.shape), _const_spec(ek.shape), _const_spec(eq.shape),
                  _const_spec(onek.shape), _const_spec(oneq.shape)],
        out_specs=[pl.BlockSpec((1, t, LANE), lambda i, j: (i, j, 0)),
                   pl.BlockSpec((t, w), lambda i, j: (i * nt + j, 0)),
                   pl.BlockSpec((t, w), lambda i, j: (i * nt + j, 0))],
        out_shape=[jax.ShapeDtypeStruct((b, l, LANE), F32),
                   jax.ShapeDtypeStruct((b * l, w), BF16),
                   jax.ShapeDtypeStruct((b * l, w), BF16)],
        scratch_shapes=[pltpu.VMEM((1, LANE), F32)],
        compiler_params=_cparams("arbitrary", "arbitrary"),
        name="cum_aug",
    )(lf_slab, init, tri, ek, eq, onek, oneq)


def _softmax_step(qs, ks, v, mask, m_ref, l_ref, acc_ref):
    tk = v.shape[0]
    for j in range(2):
        s = lax.dot_general(qs[j], ks[j], (((1,), (1,)), ((), ())), preferred_element_type=F32)
        if mask is not None:
            s = jnp.where(mask, s, NEG)
        m_prev = m_ref[j]
        m_next = jnp.maximum(m_prev, jnp.max(s, axis=1, keepdims=True))
        p = jnp.exp2(s - jnp.concatenate([m_next] * (tk // LANE), axis=1)) if tk >= LANE \
            else jnp.exp2(s - m_next[:, :tk])
        alpha = jnp.exp2(m_prev - m_next)
        l_ref[j] = alpha * l_ref[j] + jnp.sum(p, axis=1, keepdims=True)
        m_ref[j] = m_next
        acc_ref[j] = acc_ref[j] * alpha + _dot(p.astype(BF16), v)


def _init_stats(m_ref, l_ref, acc_ref):
    m_ref[...] = jnp.full(m_ref.shape, NEG, F32)
    l_ref[...] = jnp.zeros(l_ref.shape, F32)
    acc_ref[...] = jnp.zeros(acc_ref.shape, F32)


def _half_masks(shape):
    lane = lax.broadcasted_iota(jnp.int32, shape, 1)
    return (lane % LANE) < (LANE // 2)


def _queries(mode, q_vals):
    if mode == "mla":
        q = q_vals[0]
        return [q[:, :LANE], q[:, LANE:]]
    q = q_vals[0] if mode == "diff" else jnp.concatenate(q_vals, axis=1)
    q = q.astype(F32)
    first = _half_masks(q.shape)
    return [jnp.where(first, q, 0.0).astype(BF16), jnp.where(first, 0.0, q).astype(BF16)]


def _keys(mode, k_vals):
    if mode == "mla":
        k = k_vals[0]
        return [k[:, :LANE], k[:, LANE:]]
    k = k_vals[0] if mode == "diff" else jnp.concatenate(k_vals, axis=1)
    return [k, k]


def _visible(mode, q_pos, k_pos):
    if mode == "fox":
        return k_pos <= q_pos
    return (k_pos // CHUNK) <= (q_pos // CHUNK)


def _finish(mode, l_ref, acc_ref, extra, lam_init):
    o0 = acc_ref[0] / l_ref[0]
    o1 = acc_ref[1] / l_ref[1]
    if mode != "diff":
        return jnp.where(_half_masks(o0.shape), o0, o1)
    lam_ref, gsub_ref = extra
    lp = lam_ref[...]
    lam = (jnp.exp(jnp.sum(lp[0:1] * lp[1:2], axis=1, keepdims=True))
           - jnp.exp(jnp.sum(lp[2:3] * lp[3:4], axis=1, keepdims=True)) + lam_init)
    o = o0 - lam * o1
    return _rms(o, gsub_ref[...], 1e-5) * (1.0 - lam_init)


def _flash_kernel(*refs, mode, tq, tk, lam_init):
    nq_in = 2 if mode == "fox" else 1
    q_refs = refs[:nq_in]
    k_refs = refs[nq_in:2 * nq_in]
    v_ref = refs[2 * nq_in]
    n_extra = 2 if mode == "diff" else 0
    extra = refs[2 * nq_in + 1:2 * nq_in + 1 + n_extra]
    o_ref = refs[2 * nq_in + 1 + n_extra]
    m_ref, l_ref, acc_ref = refs[2 * nq_in + 2 + n_extra:]
    i = pl.program_id(2)
    _init_stats(m_ref, l_ref, acc_ref)
    qs = _queries(mode, [r[...] for r in q_refs])

    def block(start, mask):
        sl = pl.ds(start, tk)
        ks = _keys(mode, [r[sl, :] for r in k_refs])
        _softmax_step(qs, ks, v_ref[sl, :], mask, m_ref, l_ref, acc_ref)

    def body(kb, carry):
        block(pl.multiple_of(kb * tk, tk), None)
        return carry

    lax.fori_loop(0, i * (tq // tk), body, 0)
    q_pos = i * tq + lax.broadcasted_iota(jnp.int32, (tq, tk), 0)
    for d in range(tq // tk):
        start = pl.multiple_of(i * tq + d * tk, tk)
        k_pos = start + lax.broadcasted_iota(jnp.int32, (tq, tk), 1)
        block(start, _visible(mode, q_pos, k_pos))
    o_ref[...] = _finish(mode, l_ref, acc_ref, extra, lam_init).astype(o_ref.dtype)


def _flash(mode, qs, ks, v, extra, b, s, tq, tk, lam_init=0.0):
    n = b * s
    p = v.shape[1] // LANE
    nq = s // tq
    wq = [a.shape[1] // p for a in qs]
    wk = [a.shape[1] // p for a in ks]
    in_specs = ([pl.BlockSpec((tq, w), lambda bi, pi, qi: (bi * nq + qi, pi)) for w in wq]
                + [pl.BlockSpec((s, w), lambda bi, pi, qi: (bi, pi)) for w in wk]
                + [pl.BlockSpec((s, LANE), lambda bi, pi, qi: (bi, pi))]
                + [_const_spec(e.shape) for e in extra])
    return pl.pallas_call(
        functools.partial(_flash_kernel, mode=mode, tq=tq, tk=tk, lam_init=lam_init),
        grid=(b, p, nq),
        in_specs=in_specs,
        out_specs=pl.BlockSpec((tq, LANE), lambda bi, pi, qi: (bi * nq + qi, pi)),
        out_shape=jax.ShapeDtypeStruct((n, p * LANE), BF16),
        scratch_shapes=[pltpu.VMEM((2, tq, LANE), F32)] * 3,
        compiler_params=_cparams("parallel", "parallel", "arbitrary"),
        name="flash_" + mode,
    )(*qs, *ks, v, *extra)


def _sample_kernel(*refs, mode, past, tq, tkp, lam_init):
    nq_in = 2 if mode == "fox" else 1
    q_refs = refs[:nq_in]
    kp_refs = refs[nq_in:2 * nq_in]
    kn_refs = refs[2 * nq_in:3 * nq_in]
    vp_ref, vn_ref = refs[3 * nq_in:3 * nq_in + 2]
    n_extra = 2 if mode == "diff" else 0
    extra = refs[3 * nq_in + 2:3 * nq_in + 2 + n_extra]
    o_ref = refs[3 * nq_in + 2 + n_extra]
    m_ref, l_ref, acc_ref = refs[3 * nq_in + 3 + n_extra:]
    _init_stats(m_ref, l_ref, acc_ref)
    new = lambda r: (r[0] if len(r.shape) == 3 else r[...])[:tq]
    qs = _queries(mode, [new(r) for r in q_refs])
    for c in range(past // tkp):
        sl = slice(c * tkp, (c + 1) * tkp)
        ks = _keys(mode, [r[0, sl, :].astype(BF16) for r in kp_refs])
        _softmax_step(qs, ks, vp_ref[0, sl, :].astype(BF16), None, m_ref, l_ref, acc_ref)
    pad = lambda a: jnp.concatenate([a, jnp.zeros((LANE - tq, a.shape[1]), a.dtype)], axis=0)
    q_pos = past + lax.broadcasted_iota(jnp.int32, (tq, LANE), 0)
    k_idx = lax.broadcasted_iota(jnp.int32, (tq, LANE), 1)
    mask = _visible(mode, q_pos, past + k_idx) & (k_idx < tq)
    _softmax_step(qs, _keys(mode, [pad(new(r)) for r in kn_refs]), pad(vn_ref[...]), mask,
                  m_ref, l_ref, acc_ref)
    o_ref[...] = _finish(mode, l_ref, acc_ref, extra, lam_init).astype(o_ref.dtype)


def _sample_attn(mode, qs, kps, kns, vp, vn, extra, bs, tq, past, lam_init=0.0):
    p = vn.shape[1] // LANE
    tkp = min(past, 1024)

    def new_spec(a):
        w = a.shape[-1] // p
        if a.ndim == 3:
            return pl.BlockSpec((1, a.shape[1], w), lambda bi, pi: (bi, 0, pi))
        return pl.BlockSpec((tq, w), lambda bi, pi: (bi, pi))

    def past_spec(a):
        return pl.BlockSpec((1, past, a.shape[-1] // p), lambda bi, pi: (bi, 0, pi))

    in_specs = ([new_spec(a) for a in qs] + [past_spec(a) for a in kps] + [new_spec(a) for a in kns]
                + [past_spec(vp), new_spec(vn)] + [_const_spec(e.shape) for e in extra])
    return pl.pallas_call(
        functools.partial(_sample_kernel, mode=mode, past=past, tq=tq, tkp=tkp, lam_init=lam_init),
        grid=(bs, p),
        in_specs=in_specs,
        out_specs=pl.BlockSpec((tq, LANE), lambda bi, pi: (bi, pi)),
        out_shape=jax.ShapeDtypeStruct((bs * tq, p * LANE), BF16),
        scratch_shapes=[pltpu.VMEM((2, tq, LANE), F32)] * 3,
        compiler_params=_cparams("parallel", "parallel"),
        name="sample_" + mode,
    )(*qs, *kps, *kns, vp, vn, *extra)


def _kv_up_kernel(ckv_ref, kr_ref, wkv_ref, k_o, v_o):
    kin = jnp.concatenate([ckv_ref[...].astype(BF16), kr_ref[...].astype(BF16)], axis=1)
    kv = _dot(kin, wkv_ref[...])
    k_o[...] = kv[:, :H_MLA * LANE].astype(BF16)
    v_o[...] = kv[:, H_MLA * LANE:].astype(BF16)


def _kv_up(ckv, kr_slab, wkv_p, tm):
    n = ckv.shape[0]
    return pl.pallas_call(
        _kv_up_kernel,
        grid=(n // tm,),
        in_specs=[pl.BlockSpec((tm, KV_LORA), lambda i: (i, 0)),
                  pl.BlockSpec((tm, LANE), lambda i: (i, 0)),
                  _const_spec(wkv_p.shape)],
        out_specs=[pl.BlockSpec((tm, H_MLA * LANE), lambda i: (i, 0)),
                   pl.BlockSpec((tm, H_MLA * MLA_V), lambda i: (i, 0))],
        out_shape=[jax.ShapeDtypeStruct((n, H_MLA * LANE), BF16),
                   jax.ShapeDtypeStruct((n, H_MLA * MLA_V), BF16)],
        compiler_params=_cparams("parallel"),
        name="kv_up",
    )(ckv, kr_slab, wkv_p)


def _mlp_kernel(*refs, n_o, final, ff_chunk):
    o_refs = refs[:n_o]
    x_ref, wout_ref, gmlp_ref, wup_ref, wdown_ref = refs[n_o:n_o + 5]
    gfin_ref = refs[n_o + 5] if final else None
    out_ref = refs[-1]
    proj = None
    r0 = 0
    for o_ref in o_refs:
        w = o_ref.shape[1]
        t = _dot(o_ref[...], wout_ref[r0:r0 + w, :])
        proj = t if proj is None else proj + t
        r0 += w
    x1 = x_ref[...] + proj
    h = _rms(x1, gmlp_ref[...], 1e-6).astype(BF16)
    down = None
    for c in range(wup_ref.shape[1] // ff_chunk):
        u = _dot(h, wup_ref[:, c * ff_chunk:(c + 1) * ff_chunk])
        a = jnp.square(jnp.maximum(u, 0.0)).astype(BF16)
        t = _dot(a, wdown_ref[c * ff_chunk:(c + 1) * ff_chunk, :])
        down = t if down is None else down + t
    x2 = x1 + down
    out_ref[...] = _rms(x2, gfin_ref[...], 1e-6) if final else x2


def _outproj_mlp(os_, x, wout, gmlp, wup, wdown, gfin, tm):
    n, d = x.shape
    final = gfin is not None
    consts = [wout, gmlp, wup, wdown] + ([gfin] if final else [])
    return pl.pallas_call(
        functools.partial(_mlp_kernel, n_o=len(os_), final=final, ff_chunk=1024),
        grid=(n // tm,),
        in_specs=([pl.BlockSpec((tm, o.shape[1]), lambda i: (i, 0)) for o in os_]
                  + [pl.BlockSpec((tm, d), lambda i: (i, 0))]
                  + [_const_spec(c.shape) for c in consts]),
        out_specs=pl.BlockSpec((tm, d), lambda i: (i, 0)),
        out_shape=jax.ShapeDtypeStruct((n, d), F32),
        compiler_params=_cparams("parallel"),
        name="outproj_mlp",
    )(*os_, x, *consts)


def _proj_c_kernel(x_ref, tab_ref, gmix_ref, w_ref, k_o, v_o, qb_o, kb_o, vb_o):
    h = _rms(x_ref[...], gmix_ref[...], 1e-6).astype(BF16)
    tab = tab_ref[...]
    cw = H_DIFF * 2 * DIFF_DH
    qb_o[...] = _rope_wide(_dot(h, w_ref[:, :cw]), tab, 0, DIFF_ROT // 2).astype(BF16)
    k = _rope_wide(_dot(h, w_ref[:, cw:2 * cw]), tab, 3 * LANE, DIFF_ROT // 2)
    k_o[...] = k
    kb_o[...] = k.astype(BF16)
    v = _dot(h, w_ref[:, 2 * cw:])
    v_o[...] = v
    vb_o[...] = v.astype(BF16)


def _proj_c(x, tab, tab_period, gmix, w, tm):
    n, d = x.shape
    nper = tab_period // tm
    cw = H_DIFF * 2 * DIFF_DH
    row = lambda: pl.BlockSpec((tm, cw), lambda i: (i, 0))
    return pl.pallas_call(
        _proj_c_kernel,
        grid=(n // tm,),
        in_specs=[pl.BlockSpec((tm, d), lambda i: (i, 0)),
                  pl.BlockSpec((tm, 6 * LANE), lambda i: (i % nper, 0)),
                  _const_spec(gmix.shape), _const_spec(w.shape)],
        out_specs=[row() for _ in range(5)],
        out_shape=[jax.ShapeDtypeStruct((n, cw), t) for t in (F32, F32, BF16, BF16, BF16)],
        compiler_params=_cparams("parallel"),
        name="proj_c",
    )(x, tab, gmix, w)


def _angles(pos, half):
    inv = ROPE_THETA ** (-jnp.arange(half, dtype=F32) / half)
    ang = pos.astype(F32)[:, None] * inv[None, :]
    return jnp.cos(ang), jnp.sin(ang)


def _tables_a(pos):
    cos, sin = _angles(pos, MLA_ROPE // 2)
    t = pos.shape[0]
    z = lambda w: jnp.zeros((t, w), F32)
    sc = (MLA_NOPE + MLA_ROPE) ** -0.5 * LOG2E
    cq = sc * jnp.concatenate([jnp.ones((t, MLA_NOPE), F32), cos, cos, z(32)], axis=1)
    s1q = sc * jnp.concatenate([z(MLA_NOPE + 16), sin, z(32)], axis=1)
    s2q = sc * jnp.concatenate([z(MLA_NOPE), -sin, z(48)], axis=1)
    ck = jnp.concatenate([cos, cos, z(96)], axis=1)
    s1k = jnp.concatenate([z(16), sin, z(96)], axis=1)
    s2k = jnp.concatenate([-sin, z(112)], axis=1)
    return jnp.concatenate([cq, s1q, s2q, ck, s1k, s2k], axis=1)


def _tables_c(pos):
    cos, sin = _angles(pos, DIFF_ROT // 2)
    t = pos.shape[0]
    z = lambda w: jnp.zeros((t, w), F32)
    two = lambda a: jnp.concatenate([a, a], axis=1)
    c = two(jnp.concatenate([cos, cos, jnp.ones((t, 48), F32)], axis=1))
    s1 = two(jnp.concatenate([z(8), sin, z(48)], axis=1))
    s2 = two(jnp.concatenate([-sin, z(56)], axis=1))
    sc = DIFF_DH ** -0.5 * LOG2E
    return jnp.concatenate([sc * c, sc * s1, sc * s2, c, s1, s2], axis=1)


def _pad_cols(a, w):
    return jnp.pad(a, ((0, 0), (0, w - a.shape[1])))


def _prep_a(w_in, w_uq, w_ukv, b_f):
    o1, o2, o3 = Q_LORA, Q_LORA + KV_LORA, Q_LORA + KV_LORA + MLA_ROPE
    o4, o5, o6 = o3 + 512, o3 + 1024, o3 + 1536
    win_p = jnp.concatenate(
        [w_in[:, :o2], w_in[:, o3:o6], _pad_cols(w_in[:, o2:o3], LANE), _pad_cols(w_in[:, o6:], LANE)],
        axis=1).astype(BF16)
    dq = MLA_NOPE + MLA_ROPE
    wuq_p = jnp.pad(w_uq.reshape(Q_LORA, H_MLA, dq), ((0, 0), (0, 0), (0, LANE - dq)))
    wuq_p = wuq_p.reshape(Q_LORA, H_MLA * LANE).astype(BF16)
    wkv3 = w_ukv.reshape(KV_LORA, H_MLA, MLA_NOPE + MLA_V)
    wk = jnp.pad(wkv3[:, :, :MLA_NOPE], ((0, 0), (0, 0), (0, LANE - MLA_NOPE))).reshape(KV_LORA, H_MLA * LANE)
    wv = wkv3[:, :, MLA_NOPE:].reshape(KV_LORA, H_MLA * MLA_V)
    place = jnp.pad(jnp.eye(MLA_ROPE, dtype=F32), ((0, LANE - MLA_ROPE), (MLA_NOPE, LANE - MLA_NOPE - MLA_ROPE)))
    place = jnp.concatenate([jnp.tile(place, (1, H_MLA)), jnp.zeros((LANE, H_MLA * MLA_V), F32)], axis=1)
    wkv_p = jnp.concatenate([jnp.concatenate([wk, wv], axis=1), place], axis=0).astype(BF16)
    bf_slab = _pad_cols(b_f[None, :], LANE)
    return win_p, wuq_p, wkv_p, bf_slab


def _row_tile(n, want):
    t = min(n, want)
    while n % t:
        t //= 2
    return t


def kernel(x_prompt, x_sample, cache_mla_ckv, cache_mla_krope, cache_fox_k, cache_fox_v, cache_fox_logf,
           cache_diff_k, cache_diff_v, g_mix, a_w_in, a_g_q, a_w_uq, a_g_kv, a_w_ukv, a_b_f, a_w_out,
           c_w_in, c_lam, c_g_sub, c_w_out, g_mlp, w_up, w_down, g_final):
    b, s, d = x_prompt.shape
    bs, ts, _ = x_sample.shape
    past = cache_mla_ckv.shape[2]
    n, ns = b * s, bs * ts
    tm = _row_tile(s, 512)
    tq = tk = _row_tile(s, 512)
    tms = _row_tile(ns, 512)
    lam_init = 0.8 - 0.6 * math.exp(-0.3 * 1)
    row = lambda a: a[None, :]

    xp = x_prompt.reshape(n, d)
    xs = x_sample.reshape(ns, d)
    pos_p = jnp.arange(s)
    pos_s = past + jnp.arange(ts)

    win_p, wuq_p, wkv_p, bf_slab = _prep_a(a_w_in[0], a_w_uq[0], a_w_ukv[0], a_b_f[0])
    wa = (row(g_mix[0]), win_p, row(a_g_q[0]), wuq_p, row(a_g_kv[0]), wkv_p, bf_slab)
    tab_ap = _tables_a(pos_p)
    tab_as = jnp.tile(_tables_a(pos_s), (tms // ts, 1))
    (ckv_p, krope_p, fk_p, fv_p, logf_p, lfs_p, qm_p, km_p, vm_p, fq_p, fkb_p, fvb_p) = _proj_a(
        xp, tab_ap, s, *wa, tm)
    (ckv_s, krope_s, fk_s, fv_s, logf_s, lfs_s, qm_s, km_s, vm_s, fq_s, fkb_s, fvb_s) = _proj_a(
        xs, tab_as, tms, *wa, tms)

    _, kaug_p, qaug_p = _cum_aug(lfs_p.reshape(b, s, LANE), jnp.zeros((b, 1, LANE), F32), tm)
    om_p = _flash("mla", [qm_p], [km_p], vm_p, [], b, s, tq, tk)
    of_p = _flash("fox", [fq_p, qaug_p], [fkb_p, kaug_p], fvb_p, [], b, s, tq, tk)

    w4 = (H_FOX // 2) * LANE
    lf_past = jnp.pad(cache_fox_logf[0], ((0, 0), (0, 0), (0, LANE - H_FOX)))
    cum_past, kaug_past, _ = _cum_aug(lf_past, jnp.zeros((bs, 1, LANE), F32), _row_tile(past, 512))
    rpad = 128
    lf_new = jnp.pad(lfs_s.reshape(bs, ts, LANE), ((0, 0), (0, rpad - ts), (0, 0)))
    _, kaug_new, qaug_new = _cum_aug(lf_new, cum_past[:, past - 1:past, :], rpad)
    kr_past = jnp.pad(cache_mla_krope[0].reshape(bs * past, MLA_ROPE), ((0, 0), (0, LANE - MLA_ROPE)))
    km_past, vm_past = _kv_up(cache_mla_ckv[0].reshape(bs * past, KV_LORA), kr_past, wkv_p,
                              _row_tile(bs * past, 1024))
    om_s = _sample_attn("mla", [qm_s], [km_past.reshape(bs, past, -1)], [km_s],
                        vm_past.reshape(bs, past, -1), vm_s, [], bs, ts, past)
    of_s = _sample_attn("fox", [fq_s, qaug_new.reshape(bs, rpad, w4)],
                        [cache_fox_k[0].reshape(bs, past, -1), kaug_past.reshape(bs, past, w4)],
                        [fkb_s, kaug_new.reshape(bs, rpad, w4)],
                        cache_fox_v[0].reshape(bs, past, -1), fvb_s, [], bs, ts, past)

    wout_a = a_w_out[0].astype(BF16)
    wup0, wdown0 = w_up[0].astype(BF16), w_down[0].astype(BF16)
    x1_p = _outproj_mlp([om_p, of_p], xp, wout_a, row(g_mlp[0]), wup0, wdown0, None, tm)
    x1_s = _outproj_mlp([om_s, of_s], xs, wout_a, row(g_mlp[0]), wup0, wdown0, None, tms)

    wc = c_w_in[0].astype(BF16)
    tab_cp = _tables_c(pos_p)
    tab_cs = jnp.tile(_tables_c(pos_s), (tms // ts, 1))
    dk_p, dv_p, dqb_p, dkb_p, dvb_p = _proj_c(x1_p, tab_cp, s, row(g_mix[1]), wc, tm)
    dk_s, dv_s, dqb_s, dkb_s, dvb_s = _proj_c(x1_s, tab_cs, tms, row(g_mix[1]), wc, tms)
    extra = [c_lam[0], row(c_g_sub[0])]
    oc_p = _flash("diff", [dqb_p], [dkb_p], dvb_p, extra, b, s, tq, tk, lam_init)
    oc_s = _sample_attn("diff", [dqb_s], [cache_diff_k[0].reshape(bs, past, -1)], [dkb_s],
                        cache_diff_v[0].reshape(bs, past, -1), dvb_s, extra, bs, ts, past, lam_init)
    wout_c = c_w_out[0].astype(BF16)
    wup1, wdown1 = w_up[1].astype(BF16), w_down[1].astype(BF16)
    y_p = _outproj_mlp([oc_p], x1_p, wout_c, row(g_mlp[1]), wup1, wdown1, row(g_final), tm)
    y_s = _outproj_mlp([oc_s], x1_s, wout_c, row(g_mlp[1]), wup1, wdown1, row(g_final), tms)

    def pack(bb, tt, ckv, krope, fk, fv, logf, dk, dv):
        return (ckv.reshape(1, bb, tt, KV_LORA), krope.reshape(1, bb, tt, MLA_ROPE),
                fk.reshape(1, bb, tt, H_FOX, FOX_DH), fv.reshape(1, bb, tt, H_FOX, FOX_DH),
                logf.reshape(1, bb, tt, H_FOX),
                dk.reshape(1, bb, tt, H_DIFF, 2 * DIFF_DH), dv.reshape(1, bb, tt, H_DIFF, 2 * DIFF_DH))

    return ((y_p.reshape(b, s, d), y_s.reshape(bs, ts, d))
            + pack(b, s, ckv_p, krope_p, fk_p, fv_p, logf_p, dk_p, dv_p)
            + pack(bs, ts, ckv_s, krope_s, fk_s, fv_s, logf_s, dk_s, dv_s))
```

```python
import functools
import math

import jax
import jax.numpy as jnp
from jax import lax
from jax.experimental import pallas as pl
from jax.experimental.pallas import tpu as pltpu

F32 = jnp.float32
BF16 = jnp.bfloat16

CHUNK = 64
ROPE_THETA = 500000.0
H_MLA, MLA_NOPE, MLA_ROPE, MLA_V = 8, 64, 32, 64
Q_LORA, KV_LORA = 512, 256
H_FOX, FOX_DH = 8, 64
H_DIFF, DIFF_DH = 8, 64
DIFF_ROT = DIFF_DH // 4
LANE = 128
LOG2E = math.log2(math.e)
NEG = -1e30
VMEM_LIMIT = 56 * 1024 * 1024


def _cparams(*sem):
    return pltpu.CompilerParams(dimension_semantics=sem, vmem_limit_bytes=VMEM_LIMIT)


def _const_spec(shape):
    nd = len(shape)
    return pl.BlockSpec(shape, lambda *_: (0,) * nd, pipeline_mode=pl.Buffered(1))


def _rms(x, g, eps):
    y = x * lax.rsqrt(jnp.mean(x * x, axis=-1, keepdims=True) + eps)
    return y * g


def _dot(a, b):
    return jnp.dot(a, b, preferred_element_type=F32)


def _rope_slab(x, c, s1, s2, half):
    return x * c + pltpu.roll(x, half, 1) * s1 + pltpu.roll(x, LANE - half, 1) * s2


def _rope_wide(x, tab, base, half):
    c = tab[:, base:base + LANE]
    s1 = tab[:, base + LANE:base + 2 * LANE]
    s2 = tab[:, base + 2 * LANE:base + 3 * LANE]
    n = x.shape[1] // LANE
    return jnp.concatenate(
        [_rope_slab(x[:, j * LANE:(j + 1) * LANE], c, s1, s2, half) for j in range(n)], axis=1)


def _split3(x):
    hi = x.astype(BF16)
    r1 = x - hi.astype(F32)
    mid = r1.astype(BF16)
    lo = (r1 - mid.astype(F32)).astype(BF16)
    return hi, mid, lo


_O_Q, _O_CKV, _O_FQ, _O_FK, _O_FV, _O_KR, _O_F, _A_COLS = 0, 512, 768, 1280, 1792, 2304, 2432, 2560


def _store_v(v_o, v, vt):
    if not vt:
        v_o[...] = v.astype(BF16)
        return
    for p in range(v.shape[1] // LANE):
        v_o[p, 0] = v[:, p * LANE:(p + 1) * LANE].T.astype(BF16)


def _v_out(n, width, tm, vt):
    if not vt:
        return pl.BlockSpec((tm, width), lambda i: (i, 0)), jax.ShapeDtypeStruct((n, width), BF16)
    p = width // LANE
    return (pl.BlockSpec((p, 1, LANE, tm), lambda i: (0, i, 0, 0)),
            jax.ShapeDtypeStruct((p, n // tm, LANE, tm), BF16))


def _proj_a_kernel(x_ref, tab_ref, gmix_ref, win_ref, gq_ref, wuq_ref, gkv_ref, wkv_ref, bf_ref,
                   ckv_o, krope_o, fk_o, fv_o, logf_o, lfs_o, qm_o, km_o, vm_o, fq_o, fkb_o, fvb_o,
                   *, fox_scale, vt):
    h = _rms(x_ref[...], gmix_ref[...], 1e-6).astype(BF16)
    tab = tab_ref[...]
    ql = _dot(h, win_ref[:, _O_Q:_O_Q + Q_LORA])
    qn = _rms(ql, gq_ref[...], 1e-6).astype(BF16)
    q = _dot(qn, wuq_ref[...])
    qm_o[...] = _rope_wide(q, tab, 0, MLA_ROPE // 2).astype(BF16)
    cn = _rms(_dot(h, win_ref[:, _O_CKV:_O_CKV + KV_LORA]), gkv_ref[...], 1e-6)
    ckv_o[...] = cn
    kr = _rope_wide(_dot(h, win_ref[:, _O_KR:_O_KR + LANE]), tab, 3 * LANE, MLA_ROPE // 2)
    krope_o[...] = kr[:, :MLA_ROPE]
    kin = jnp.concatenate([cn.astype(BF16), kr.astype(BF16)], axis=1)
    kv = _dot(kin, wkv_ref[...])
    km_o[...] = kv[:, :H_MLA * LANE].astype(BF16)
    _store_v(vm_o, kv[:, H_MLA * LANE:], vt)
    fq_o[...] = (_dot(h, win_ref[:, _O_FQ:_O_FQ + 512]) * fox_scale).astype(BF16)
    fk = _dot(h, win_ref[:, _O_FK:_O_FK + 512])
    fk_o[...] = fk
    fkb_o[...] = fk.astype(BF16)
    fv = _dot(h, win_ref[:, _O_FV:_O_FV + 512])
    fv_o[...] = fv
    _store_v(fvb_o, fv, vt)
    z = _dot(h, win_ref[:, _O_F:_O_F + LANE]) + bf_ref[...]
    ls = jnp.minimum(z, 0.0) - jnp.log1p(jnp.exp(-jnp.abs(z)))
    lane = lax.broadcasted_iota(jnp.int32, ls.shape, 1)
    ls = jnp.where(lane < H_FOX, ls, 0.0)
    lfs_o[...] = ls
    logf_o[...] = ls[:, :H_FOX]


def _proj_a(x, tab, tab_period, gmix, win_p, gq, wuq_p, gkv, wkv_p, bf_slab, tm, vt):
    n, d = x.shape
    nt = n // tm
    nper = tab_period // tm
    row = lambda w: pl.BlockSpec((tm, w), lambda i: (i, 0))
    plain = lambda w, t: (row(w), jax.ShapeDtypeStruct((n, w), t))
    outs = [plain(KV_LORA, F32), plain(MLA_ROPE, F32), plain(512, F32), plain(512, F32),
            plain(H_FOX, F32), plain(LANE, F32),
            plain(H_MLA * LANE, BF16), plain(H_MLA * LANE, BF16), _v_out(n, H_MLA * MLA_V, tm, vt),
            plain(512, BF16), plain(512, BF16), _v_out(n, 512, tm, vt)]
    return pl.pallas_call(
        functools.partial(_proj_a_kernel, fox_scale=FOX_DH ** -0.5 * LOG2E, vt=vt),
        grid=(nt,),
        in_specs=[row(d),
                  pl.BlockSpec((tm, 6 * LANE), lambda i: (i % nper, 0)),
                  _const_spec(gmix.shape), _const_spec(win_p.shape), _const_spec(gq.shape),
                  _const_spec(wuq_p.shape), _const_spec(gkv.shape), _const_spec(wkv_p.shape),
                  _const_spec(bf_slab.shape)],
        out_specs=[o[0] for o in outs],
        out_shape=[o[1] for o in outs],
        compiler_params=_cparams("parallel"),
        name="proj_a",
    )(x, tab, gmix, win_p, gq, wuq_p, gkv, wkv_p, bf_slab)


def _cum_kernel(lf_ref, init_ref, tri_ref, ek_ref, eq_ref, onek_ref, oneq_ref,
                cum_o, kaug_o, qaug_o, carry):
    @pl.when(pl.program_id(1) == 0)
    def _():
        carry[...] = init_ref[0]

    tri = tri_ref[...]
    hi, mid, lo = _split3(lf_ref[0])
    cum = carry[...] + (_dot(tri, hi) + _dot(tri, mid) + _dot(tri, lo))
    t = cum.shape[0]
    carry[...] = cum[t - 1:t, :]
    cum_o[0] = cum
    hi, mid, lo = _split3(cum * LOG2E)
    kaug_o[...] = (_dot(hi, ek_ref[0]) + _dot(mid, ek_ref[1]) + _dot(lo, ek_ref[2])
                   + onek_ref[...]).astype(BF16)
    qaug_o[...] = (_dot(hi, eq_ref[0]) + _dot(mid, eq_ref[1]) + _dot(lo, eq_ref[2])
                   + oneq_ref[...]).astype(BF16)


def _aug_tables():
    w = (H_FOX // 2) * LANE
    ek = [[[0.0] * w for _ in range(LANE)] for _ in range(3)]
    eq = [[[0.0] * w for _ in range(LANE)] for _ in range(3)]
    onek = [0.0] * w
    oneq = [0.0] * w
    for hh in range(H_FOX):
        base = (hh // 2) * LANE + (hh % 2) * (LANE // 2)
        for s in range(3):
            ek[s][hh][base + s] = -1.0
            eq[s][hh][base + 3 + s] = 1.0
            onek[base + 3 + s] = 1.0
            oneq[base + s] = 1.0
    return (jnp.array(ek, BF16), jnp.array(eq, BF16),
            jnp.array([onek], F32), jnp.array([oneq], F32))


def _cum_aug(lf_slab, init, t):
    b, l, _ = lf_slab.shape
    nt = l // t
    tri = (lax.broadcasted_iota(jnp.int32, (t, t), 1)
           <= lax.broadcasted_iota(jnp.int32, (t, t), 0)).astype(BF16)
    ek, eq, onek, oneq = _aug_tables()
    w = (H_FOX // 2) * LANE
    return pl.pallas_call(
        _cum_kernel,
        grid=(b, nt),
        in_specs=[pl.BlockSpec((1, t, LANE), lambda i, j: (i, j, 0)),
                  pl.BlockSpec((1, 1, LANE), lambda i, j: (i, 0, 0)),
                  _const_spec(tri.shape), _const_spec(ek.shape), _const_spec(eq.shape),
                  _const_spec(onek.shape), _const_spec(oneq.shape)],
        out_specs=[pl.BlockSpec((1, t, LANE), lambda i, j: (i, j, 0)),
                   pl.BlockSpec((t, w), lambda i, j: (i * nt + j, 0)),
                   pl.BlockSpec((t, w), lambda i, j: (i * nt + j, 0))],
        out_shape=[jax.ShapeDtypeStruct((b, l, LANE), F32),
                   jax.ShapeDtypeStruct((b * l, w), BF16),
                   jax.ShapeDtypeStruct((b * l, w), BF16)],
        scratch_shapes=[pltpu.VMEM((1, LANE), F32)],
        compiler_params=_cparams("arbitrary", "arbitrary"),
        name="cum_aug",
    )(lf_slab, init, tri, ek, eq, onek, oneq)


def _softmax_step(qs, ks, v, mask, m_ref, l_ref, acc_ref):
    tk = v.shape[0]
    for j in range(2):
        s = lax.dot_general(qs[j], ks[j], (((1,), (1,)), ((), ())), preferred_element_type=F32)
        if mask is not None:
            s = jnp.where(mask, s, NEG)
        m_prev = m_ref[j]
        m_next = jnp.maximum(m_prev, jnp.max(s, axis=1, keepdims=True))
        p = jnp.exp2(s - jnp.concatenate([m_next] * (tk // LANE), axis=1)) if tk >= LANE \
            else jnp.exp2(s - m_next[:, :tk])
        alpha = jnp.exp2(m_prev - m_next)
        l_ref[j] = alpha * l_ref[j] + jnp.sum(p, axis=1, keepdims=True)
        m_ref[j] = m_next
        acc_ref[j] = acc_ref[j] * alpha + _dot(p.astype(BF16), v)


def _init_stats(m_ref, l_ref, acc_ref):
    m_ref[...] = jnp.full(m_ref.shape, NEG, F32)
    l_ref[...] = jnp.zeros(l_ref.shape, F32)
    acc_ref[...] = jnp.zeros(acc_ref.shape, F32)


def _half_masks(shape):
    lane = lax.broadcasted_iota(jnp.int32, shape, 1)
    return (lane % LANE) < (LANE // 2)


def _queries(mode, q_vals):
    if mode == "mla":
        q = q_vals[0]
        return [q[:, :LANE], q[:, LANE:]]
    q = q_vals[0] if mode == "diff" else jnp.concatenate(q_vals, axis=1)
    q = q.astype(F32)
    first = _half_masks(q.shape)
    return [jnp.where(first, q, 0.0).astype(BF16), jnp.where(first, 0.0, q).astype(BF16)]


def _keys(mode, k_vals):
    if mode == "mla":
        k = k_vals[0]
        return [k[:, :LANE], k[:, LANE:]]
    k = k_vals[0] if mode == "diff" else jnp.concatenate(k_vals, axis=1)
    return [k, k]


def _visible(mode, q_pos, k_pos):
    if mode == "fox":
        return k_pos <= q_pos
    return (k_pos // CHUNK) <= (q_pos // CHUNK)


def _finish(mode, l_ref, acc_ref, extra, lam_init):
    o0 = acc_ref[0] / l_ref[0]
    o1 = acc_ref[1] / l_ref[1]
    if mode != "diff":
        return jnp.where(_half_masks(o0.shape), o0, o1)
    lam_ref, gsub_ref = extra
    lp = lam_ref[...]
    lam = (jnp.exp(jnp.sum(lp[0:1] * lp[1:2], axis=1, keepdims=True))
           - jnp.exp(jnp.sum(lp[2:3] * lp[3:4], axis=1, keepdims=True)) + lam_init)
    o = o0 - lam * o1
    return _rms(o, gsub_ref[...], 1e-5) * (1.0 - lam_init)


def _logits_t(qs, ks):
    return tuple(lax.dot_general(ks[j], qs[j], (((1,), (1,)), ((), ())), preferred_element_type=F32)
                 for j in range(2))


ONES_ROWS = 16


def _softmax_step_t(sts, mxs, vts, mask, m_ref, acc_ref):
    for j in range(2):
        st = sts[j]
        mx = mxs[j]
        if mask is not None:
            st = jnp.where(mask, st, NEG)
            mx = jnp.max(st, axis=0, keepdims=True)
        m_prev = m_ref[j]
        m_next = jnp.maximum(m_prev, mx)
        pt = jnp.exp2(st - m_next)
        alpha = jnp.exp2(m_prev - m_next)
        m_ref[j] = m_next
        acc_ref[j] = acc_ref[j] * alpha + _dot(vts[j], pt.astype(BF16))


def _finish_t(mode, acc_ref, extra, lam_init):
    rows = acc_ref.shape[1] - ONES_ROWS
    o0 = acc_ref[0, :rows] / acc_ref[0, rows:rows + 1]
    o1 = acc_ref[1, :rows] / acc_ref[1, rows:rows + 1]
    if mode != "diff":
        return jnp.concatenate([o0, o1], axis=0).T
    lam_ref, gsub_ref = extra
    lp = lam_ref[...]
    lam = (jnp.exp(jnp.sum(lp[0:1] * lp[1:2], axis=1, keepdims=True))
           - jnp.exp(jnp.sum(lp[2:3] * lp[3:4], axis=1, keepdims=True)) + lam_init)
    o = (o0 - lam * o1).T
    return _rms(o, gsub_ref[...], 1e-5) * (1.0 - lam_init)


def _flash_kernel(*refs, mode, tq, tk, lam_init):
    nq_in = 2 if mode == "fox" else 1
    q_refs = refs[:nq_in]
    k_refs = refs[nq_in:2 * nq_in]
    vt_ref = refs[2 * nq_in]
    n_extra = 2 if mode == "diff" else 0
    extra = refs[2 * nq_in + 1:2 * nq_in + 1 + n_extra]
    o_ref = refs[2 * nq_in + 1 + n_extra]
    m_ref, acc_ref, st_ref, mx_ref = refs[2 * nq_in + 2 + n_extra:]
    i = pl.program_id(2)
    m_ref[...] = jnp.full(m_ref.shape, NEG, F32)
    acc_ref[...] = jnp.zeros(acc_ref.shape, F32)
    qs = _queries(mode, [r[...] for r in q_refs])
    half = LANE // 2
    ones = jnp.ones((ONES_ROWS, tk), BF16)

    def produce(kb, slot):
        sl = pl.ds(pl.multiple_of(kb * tk, tk), tk)
        sts = _logits_t(qs, _keys(mode, [r[sl, :] for r in k_refs]))
        for j in range(2):
            st_ref[slot, j] = sts[j]
            mx_ref[slot, j] = jnp.max(sts[j], axis=0, keepdims=True)

    def consume(kb, slot, mask):
        vt = vt_ref[kb]
        vts = [vt, vt] if mode == "diff" else [vt[:half], vt[half:]]
        vts = [jnp.concatenate([v, ones], axis=0) for v in vts]
        _softmax_step_t([st_ref[slot, j] for j in range(2)], [mx_ref[slot, j] for j in range(2)],
                        vts, mask, m_ref, acc_ref)

    produce(0, 0)

    def pair(t, carry):
        produce(2 * t + 1, 1)
        consume(2 * t, 0, None)
        produce(2 * t + 2, 0)
        consume(2 * t + 1, 1, None)
        return carry

    lax.fori_loop(0, i // 2, pair, 0)
    q_pos = i * tq + lax.broadcasted_iota(jnp.int32, (tk, tq), 1)
    k_pos = i * tq + lax.broadcasted_iota(jnp.int32, (tk, tq), 0)
    mask = _visible(mode, q_pos, k_pos)

    @pl.when(i % 2 == 1)
    def _():
        produce(i, 1)
        consume(i - 1, 0, None)
        consume(i, 1, mask)

    @pl.when(i % 2 == 0)
    def _():
        consume(i, 0, mask)

    o_ref[...] = _finish_t(mode, acc_ref, extra, lam_init).astype(o_ref.dtype)


def _flash(mode, qs, ks, vt, extra, b, s, tq, tk, lam_init=0.0):
    n = b * s
    p = vt.shape[0]
    nq = s // tq
    wq = [a.shape[1] // p for a in qs]
    wk = [a.shape[1] // p for a in ks]
    rows = LANE if mode == "diff" else LANE // 2
    in_specs = ([pl.BlockSpec((tq, w), lambda bi, pi, qi: (bi * nq + qi, pi)) for w in wq]
                + [pl.BlockSpec((s, w), lambda bi, pi, qi: (bi, pi)) for w in wk]
                + [pl.BlockSpec((None, s // tk, LANE, tk), lambda bi, pi, qi: (pi, bi, 0, 0))]
                + [_const_spec(e.shape) for e in extra])
    return pl.pallas_call(
        functools.partial(_flash_kernel, mode=mode, tq=tq, tk=tk, lam_init=lam_init),
        grid=(b, p, nq),
        in_specs=in_specs,
        out_specs=pl.BlockSpec((tq, LANE), lambda bi, pi, qi: (bi * nq + qi, pi)),
        out_shape=jax.ShapeDtypeStruct((n, p * LANE), BF16),
        scratch_shapes=[pltpu.VMEM((2, 1, tq), F32),
                        pltpu.VMEM((2, rows + ONES_ROWS, tq), F32),
                        pltpu.VMEM((2, 2, tk, tq), F32), pltpu.VMEM((2, 2, 1, tq), F32)],
        compiler_params=_cparams("parallel", "parallel", "arbitrary"),
        name="flash_" + mode,
    )(*qs, *ks, vt, *extra)


def _sample_kernel(*refs, mode, past, tq, tkp, lam_init):
    nq_in = 2 if mode == "fox" else 1
    q_refs = refs[:nq_in]
    kp_refs = refs[nq_in:2 * nq_in]
    kn_refs = refs[2 * nq_in:3 * nq_in]
    vp_ref, vn_ref = refs[3 * nq_in:3 * nq_in + 2]
    n_extra = 2 if mode == "diff" else 0
    extra = refs[3 * nq_in + 2:3 * nq_in + 2 + n_extra]
    o_ref = refs[3 * nq_in + 2 + n_extra]
    m_ref, l_ref, acc_ref = refs[3 * nq_in + 3 + n_extra:]
    _init_stats(m_ref, l_ref, acc_ref)
    new = lambda r: (r[0] if len(r.shape) == 3 else r[...])[:tq]
    qs = _queries(mode, [new(r) for r in q_refs])
    for c in range(past // tkp):
        sl = slice(c * tkp, (c + 1) * tkp)
        ks = _keys(mode, [r[0, sl, :].astype(BF16) for r in kp_refs])
        _softmax_step(qs, ks, vp_ref[0, sl, :].astype(BF16), None, m_ref, l_ref, acc_ref)
    pad = lambda a: jnp.concatenate([a, jnp.zeros((LANE - tq, a.shape[1]), a.dtype)], axis=0)
    q_pos = past + lax.broadcasted_iota(jnp.int32, (tq, LANE), 0)
    k_idx = lax.broadcasted_iota(jnp.int32, (tq, LANE), 1)
    mask = _visible(mode, q_pos, past + k_idx) & (k_idx < tq)
    _softmax_step(qs, _keys(mode, [pad(new(r)) for r in kn_refs]), pad(vn_ref[...]), mask,
                  m_ref, l_ref, acc_ref)
    o_ref[...] = _finish(mode, l_ref, acc_ref, extra, lam_init).astype(o_ref.dtype)


def _sample_attn(mode, qs, kps, kns, vp, vn, extra, bs, tq, past, lam_init=0.0):
    p = vn.shape[1] // LANE
    tkp = min(past, 1024)

    def new_spec(a):
        w = a.shape[-1] // p
        if a.ndim == 3:
            return pl.BlockSpec((1, a.shape[1], w), lambda bi, pi: (bi, 0, pi))
        return pl.BlockSpec((tq, w), lambda bi, pi: (bi, pi))

    def past_spec(a):
        return pl.BlockSpec((1, past, a.shape[-1] // p), lambda bi, pi: (bi, 0, pi))

    in_specs = ([new_spec(a) for a in qs] + [past_spec(a) for a in kps] + [new_spec(a) for a in kns]
                + [past_spec(vp), new_spec(vn)] + [_const_spec(e.shape) for e in extra])
    return pl.pallas_call(
        functools.partial(_sample_kernel, mode=mode, past=past, tq=tq, tkp=tkp, lam_init=lam_init),
        grid=(bs, p),
        in_specs=in_specs,
        out_specs=pl.BlockSpec((tq, LANE), lambda bi, pi: (bi, pi)),
        out_shape=jax.ShapeDtypeStruct((bs * tq, p * LANE), BF16),
        scratch_shapes=[pltpu.VMEM((2, tq, LANE), F32)] * 3,
        compiler_params=_cparams("parallel", "parallel"),
        name="sample_" + mode,
    )(*qs, *kps, *kns, vp, vn, *extra)


def _kv_up_kernel(ckv_ref, kr_ref, wkv_ref, k_o, v_o):
    kin = jnp.concatenate([ckv_ref[...].astype(BF16), kr_ref[...].astype(BF16)], axis=1)
    kv = _dot(kin, wkv_ref[...])
    k_o[...] = kv[:, :H_MLA * LANE].astype(BF16)
    v_o[...] = kv[:, H_MLA * LANE:].astype(BF16)


def _kv_up(ckv, kr_slab, wkv_p, tm):
    n = ckv.shape[0]
    return pl.pallas_call(
        _kv_up_kernel,
        grid=(n // tm,),
        in_specs=[pl.BlockSpec((tm, KV_LORA), lambda i: (i, 0)),
                  pl.BlockSpec((tm, LANE), lambda i: (i, 0)),
                  _const_spec(wkv_p.shape)],
        out_specs=[pl.BlockSpec((tm, H_MLA * LANE), lambda i: (i, 0)),
                   pl.BlockSpec((tm, H_MLA * MLA_V), lambda i: (i, 0))],
        out_shape=[jax.ShapeDtypeStruct((n, H_MLA * LANE), BF16),
                   jax.ShapeDtypeStruct((n, H_MLA * MLA_V), BF16)],
        compiler_params=_cparams("parallel"),
        name="kv_up",
    )(ckv, kr_slab, wkv_p)


def _mlp_kernel(*refs, n_o, final, ff_chunk):
    o_refs = refs[:n_o]
    x_ref, wout_ref, gmlp_ref, wup_ref, wdown_ref = refs[n_o:n_o + 5]
    gfin_ref = refs[n_o + 5] if final else None
    out_ref = refs[-1]
    proj = None
    r0 = 0
    for o_ref in o_refs:
        w = o_ref.shape[1]
        t = _dot(o_ref[...], wout_ref[r0:r0 + w, :])
        proj = t if proj is None else proj + t
        r0 += w
    x1 = x_ref[...] + proj
    h = _rms(x1, gmlp_ref[...], 1e-6).astype(BF16)
    down = None
    for c in range(wup_ref.shape[1] // ff_chunk):
        u = _dot(h, wup_ref[:, c * ff_chunk:(c + 1) * ff_chunk])
        a = jnp.square(jnp.maximum(u, 0.0)).astype(BF16)
        t = _dot(a, wdown_ref[c * ff_chunk:(c + 1) * ff_chunk, :])
        down = t if down is None else down + t
    x2 = x1 + down
    out_ref[...] = _rms(x2, gfin_ref[...], 1e-6) if final else x2


def _outproj_mlp(os_, x, wout, gmlp, wup, wdown, gfin, tm):
    n, d = x.shape
    final = gfin is not None
    consts = [wout, gmlp, wup, wdown] + ([gfin] if final else [])
    return pl.pallas_call(
        functools.partial(_mlp_kernel, n_o=len(os_), final=final, ff_chunk=1024),
        grid=(n // tm,),
        in_specs=([pl.BlockSpec((tm, o.shape[1]), lambda i: (i, 0)) for o in os_]
                  + [pl.BlockSpec((tm, d), lambda i: (i, 0))]
                  + [_const_spec(c.shape) for c in consts]),
        out_specs=pl.BlockSpec((tm, d), lambda i: (i, 0)),
        out_shape=jax.ShapeDtypeStruct((n, d), F32),
        compiler_params=_cparams("parallel"),
        name="outproj_mlp",
    )(*os_, x, *consts)


def _proj_c_kernel(x_ref, tab_ref, gmix_ref, w_ref, k_o, v_o, qb_o, kb_o, vb_o, *, vt):
    h = _rms(x_ref[...], gmix_ref[...], 1e-6).astype(BF16)
    tab = tab_ref[...]
    cw = H_DIFF * 2 * DIFF_DH
    qb_o[...] = _rope_wide(_dot(h, w_ref[:, :cw]), tab, 0, DIFF_ROT // 2).astype(BF16)
    k = _rope_wide(_dot(h, w_ref[:, cw:2 * cw]), tab, 3 * LANE, DIFF_ROT // 2)
    k_o[...] = k
    kb_o[...] = k.astype(BF16)
    v = _dot(h, w_ref[:, 2 * cw:])
    v_o[...] = v
    _store_v(vb_o, v, vt)


def _proj_c(x, tab, tab_period, gmix, w, tm, vt):
    n, d = x.shape
    nper = tab_period // tm
    cw = H_DIFF * 2 * DIFF_DH
    plain = lambda t: (pl.BlockSpec((tm, cw), lambda i: (i, 0)), jax.ShapeDtypeStruct((n, cw), t))
    outs = [plain(F32), plain(F32), plain(BF16), plain(BF16), _v_out(n, cw, tm, vt)]
    return pl.pallas_call(
        functools.partial(_proj_c_kernel, vt=vt),
        grid=(n // tm,),
        in_specs=[pl.BlockSpec((tm, d), lambda i: (i, 0)),
                  pl.BlockSpec((tm, 6 * LANE), lambda i: (i % nper, 0)),
                  _const_spec(gmix.shape), _const_spec(w.shape)],
        out_specs=[o[0] for o in outs],
        out_shape=[o[1] for o in outs],
        compiler_params=_cparams("parallel"),
        name="proj_c",
    )(x, tab, gmix, w)


def _angles(pos, half):
    inv = ROPE_THETA ** (-jnp.arange(half, dtype=F32) / half)
    ang = pos.astype(F32)[:, None] * inv[None, :]
    return jnp.cos(ang), jnp.sin(ang)


def _tables_a(pos):
    cos, sin = _angles(pos, MLA_ROPE // 2)
    t = pos.shape[0]
    z = lambda w: jnp.zeros((t, w), F32)
    sc = (MLA_NOPE + MLA_ROPE) ** -0.5 * LOG2E
    cq = sc * jnp.concatenate([jnp.ones((t, MLA_NOPE), F32), cos, cos, z(32)], axis=1)
    s1q = sc * jnp.concatenate([z(MLA_NOPE + 16), sin, z(32)], axis=1)
    s2q = sc * jnp.concatenate([z(MLA_NOPE), -sin, z(48)], axis=1)
    ck = jnp.concatenate([cos, cos, z(96)], axis=1)
    s1k = jnp.concatenate([z(16), sin, z(96)], axis=1)
    s2k = jnp.concatenate([-sin, z(112)], axis=1)
    return jnp.concatenate([cq, s1q, s2q, ck, s1k, s2k], axis=1)


def _tables_c(pos):
    cos, sin = _angles(pos, DIFF_ROT // 2)
    t = pos.shape[0]
    z = lambda w: jnp.zeros((t, w), F32)
    two = lambda a: jnp.concatenate([a, a], axis=1)
    c = two(jnp.concatenate([cos, cos, jnp.ones((t, 48), F32)], axis=1))
    s1 = two(jnp.concatenate([z(8), sin, z(48)], axis=1))
    s2 = two(jnp.concatenate([-sin, z(56)], axis=1))
    sc = DIFF_DH ** -0.5 * LOG2E
    return jnp.concatenate([sc * c, sc * s1, sc * s2, c, s1, s2], axis=1)


def _pad_cols(a, w):
    return jnp.pad(a, ((0, 0), (0, w - a.shape[1])))


def _prep_a(w_in, w_uq, w_ukv, b_f):
    o1, o2, o3 = Q_LORA, Q_LORA + KV_LORA, Q_LORA + KV_LORA + MLA_ROPE
    o4, o5, o6 = o3 + 512, o3 + 1024, o3 + 1536
    win_p = jnp.concatenate(
        [w_in[:, :o2], w_in[:, o3:o6], _pad_cols(w_in[:, o2:o3], LANE), _pad_cols(w_in[:, o6:], LANE)],
        axis=1).astype(BF16)
    dq = MLA_NOPE + MLA_ROPE
    wuq_p = jnp.pad(w_uq.reshape(Q_LORA, H_MLA, dq), ((0, 0), (0, 0), (0, LANE - dq)))
    wuq_p = wuq_p.reshape(Q_LORA, H_MLA * LANE).astype(BF16)
    wkv3 = w_ukv.reshape(KV_LORA, H_MLA, MLA_NOPE + MLA_V)
    wk = jnp.pad(wkv3[:, :, :MLA_NOPE], ((0, 0), (0, 0), (0, LANE - MLA_NOPE))).reshape(KV_LORA, H_MLA * LANE)
    wv = wkv3[:, :, MLA_NOPE:].reshape(KV_LORA, H_MLA * MLA_V)
    place = jnp.pad(jnp.eye(MLA_ROPE, dtype=F32), ((0, LANE - MLA_ROPE), (MLA_NOPE, LANE - MLA_NOPE - MLA_ROPE)))
    place = jnp.concatenate([jnp.tile(place, (1, H_MLA)), jnp.zeros((LANE, H_MLA * MLA_V), F32)], axis=1)
    wkv_p = jnp.concatenate([jnp.concatenate([wk, wv], axis=1), place], axis=0).astype(BF16)
    bf_slab = _pad_cols(b_f[None, :], LANE)
    return win_p, wuq_p, wkv_p, bf_slab


def _row_tile(n, want):
    t = min(n, want)
    while n % t:
        t //= 2
    return t


def kernel(x_prompt, x_sample, cache_mla_ckv, cache_mla_krope, cache_fox_k, cache_fox_v, cache_fox_logf,
           cache_diff_k, cache_diff_v, g_mix, a_w_in, a_g_q, a_w_uq, a_g_kv, a_w_ukv, a_b_f, a_w_out,
           c_w_in, c_lam, c_g_sub, c_w_out, g_mlp, w_up, w_down, g_final):
    b, s, d = x_prompt.shape
    bs, ts, _ = x_sample.shape
    past = cache_mla_ckv.shape[2]
    n, ns = b * s, bs * ts
    tm = _row_tile(s, 512)
    tq = tk = tm
    tms = _row_tile(ns, 512)
    lam_init = 0.8 - 0.6 * math.exp(-0.3 * 1)
    row = lambda a: a[None, :]

    xp = x_prompt.reshape(n, d)
    xs = x_sample.reshape(ns, d)
    pos_p = jnp.arange(s)
    pos_s = past + jnp.arange(ts)

    win_p, wuq_p, wkv_p, bf_slab = _prep_a(a_w_in[0], a_w_uq[0], a_w_ukv[0], a_b_f[0])
    wa = (row(g_mix[0]), win_p, row(a_g_q[0]), wuq_p, row(a_g_kv[0]), wkv_p, bf_slab)
    tab_ap = _tables_a(pos_p)
    tab_as = jnp.tile(_tables_a(pos_s), (tms // ts, 1))
    (ckv_p, krope_p, fk_p, fv_p, logf_p, lfs_p, qm_p, km_p, vm_p, fq_p, fkb_p, fvb_p) = _proj_a(
        xp, tab_ap, s, *wa, tm, True)
    (ckv_s, krope_s, fk_s, fv_s, logf_s, lfs_s, qm_s, km_s, vm_s, fq_s, fkb_s, fvb_s) = _proj_a(
        xs, tab_as, tms, *wa, tms, False)

    _, kaug_p, qaug_p = _cum_aug(lfs_p.reshape(b, s, LANE), jnp.zeros((b, 1, LANE), F32), tm)
    om_p = _flash("mla", [qm_p], [km_p], vm_p, [], b, s, tq, tk)
    of_p = _flash("fox", [fq_p, qaug_p], [fkb_p, kaug_p], fvb_p, [], b, s, tq, tk)

    w4 = (H_FOX // 2) * LANE
    lf_past = jnp.pad(cache_fox_logf[0], ((0, 0), (0, 0), (0, LANE - H_FOX)))
    cum_past, kaug_past, _ = _cum_aug(lf_past, jnp.zeros((bs, 1, LANE), F32), _row_tile(past, 512))
    rpad = 128
    lf_new = jnp.pad(lfs_s.reshape(bs, ts, LANE), ((0, 0), (0, rpad - ts), (0, 0)))
    _, kaug_new, qaug_new = _cum_aug(lf_new, cum_past[:, past - 1:past, :], rpad)
    kr_past = jnp.pad(cache_mla_krope[0].reshape(bs * past, MLA_ROPE), ((0, 0), (0, LANE - MLA_ROPE)))
    km_past, vm_past = _kv_up(cache_mla_ckv[0].reshape(bs * past, KV_LORA), kr_past, wkv_p,
                              _row_tile(bs * past, 1024))
    om_s = _sample_attn("mla", [qm_s], [km_past.reshape(bs, past, -1)], [km_s],
                        vm_past.reshape(bs, past, -1), vm_s, [], bs, ts, past)
    of_s = _sample_attn("fox", [fq_s, qaug_new.reshape(bs, rpad, w4)],
                        [cache_fox_k[0].reshape(bs, past, -1), kaug_past.reshape(bs, past, w4)],
                        [fkb_s, kaug_new.reshape(bs, rpad, w4)],
                        cache_fox_v[0].reshape(bs, past, -1), fvb_s, [], bs, ts, past)

    wout_a = a_w_out[0].astype(BF16)
    wup0, wdown0 = w_up[0].astype(BF16), w_down[0].astype(BF16)
    x1_p = _outproj_mlp([om_p, of_p], xp, wout_a, row(g_mlp[0]), wup0, wdown0, None, tm)
    x1_s = _outproj_mlp([om_s, of_s], xs, wout_a, row(g_mlp[0]), wup0, wdown0, None, tms)

    wc = c_w_in[0].astype(BF16)
    tab_cp = _tables_c(pos_p)
    tab_cs = jnp.tile(_tables_c(pos_s), (tms // ts, 1))
    dk_p, dv_p, dqb_p, dkb_p, dvb_p = _proj_c(x1_p, tab_cp, s, row(g_mix[1]), wc, tm, True)
    dk_s, dv_s, dqb_s, dkb_s, dvb_s = _proj_c(x1_s, tab_cs, tms, row(g_mix[1]), wc, tms, False)
    extra = [c_lam[0], row(c_g_sub[0])]
    oc_p = _flash("diff", [dqb_p], [dkb_p], dvb_p, extra, b, s, tq, tk, lam_init)
    oc_s = _sample_attn("diff", [dqb_s], [cache_diff_k[0].reshape(bs, past, -1)], [dkb_s],
                        cache_diff_v[0].reshape(bs, past, -1), dvb_s, extra, bs, ts, past, lam_init)
    wout_c = c_w_out[0].astype(BF16)
    wup1, wdown1 = w_up[1].astype(BF16), w_down[1].astype(BF16)
    y_p = _outproj_mlp([oc_p], x1_p, wout_c, row(g_mlp[1]), wup1, wdown1, row(g_final), tm)
    y_s = _outproj_mlp([oc_s], x1_s, wout_c, row(g_mlp[1]), wup1, wdown1, row(g_final), tms)

    def pack(bb, tt, ckv, krope, fk, fv, logf, dk, dv):
        return (ckv.reshape(1, bb, tt, KV_LORA), krope.reshape(1, bb, tt, MLA_ROPE),
                fk.reshape(1, bb, tt, H_FOX, FOX_DH), fv.reshape(1, bb, tt, H_FOX, FOX_DH),
                logf.reshape(1, bb, tt, H_FOX),
                dk.reshape(1, bb, tt, H_DIFF, 2 * DIFF_DH), dv.reshape(1, bb, tt, H_DIFF, 2 * DIFF_DH))

    return ((y_p.reshape(b, s, d), y_s.reshape(bs, ts, d))
            + pack(b, s, ckv_p, krope_p, fk_p, fv_p, logf_p, dk_p, dv_p)
            + pack(bs, ts, ckv_s, krope_s, fk_s, fv_s, logf_s, dk_s, dv_s))
```

```python
import functools
import math

import jax
import jax.numpy as jnp
from jax import lax
from jax.experimental import pallas as pl
from jax.experimental.pallas import tpu as pltpu

F32 = jnp.float32
BF16 = jnp.bfloat16

CHUNK = 64
ROPE_THETA = 500000.0
H_MLA, MLA_NOPE, MLA_ROPE, MLA_V = 8, 64, 32, 64
Q_LORA, KV_LORA = 512, 256
H_FOX, FOX_DH = 8, 64
H_DIFF, DIFF_DH = 8, 64
DIFF_ROT = DIFF_DH // 4
LANE = 128
LOG2E = math.log2(math.e)
NEG = -1e30
VMEM_LIMIT = 56 * 1024 * 1024


def _cparams(*sem):
    return pltpu.CompilerParams(dimension_semantics=sem, vmem_limit_bytes=VMEM_LIMIT)


def _const_spec(shape):
    nd = len(shape)
    return pl.BlockSpec(shape, lambda *_: (0,) * nd, pipeline_mode=pl.Buffered(1))


def _rms(x, g, eps):
    y = x * lax.rsqrt(jnp.mean(x * x, axis=-1, keepdims=True) + eps)
    return y * g


def _dot(a, b):
    return jnp.dot(a, b, preferred_element_type=F32)


def _rope_slab(x, c, s1, s2, half):
    return x * c + pltpu.roll(x, half, 1) * s1 + pltpu.roll(x, LANE - half, 1) * s2


def _rope_wide(x, tab, base, half):
    c = tab[:, base:base + LANE]
    s1 = tab[:, base + LANE:base + 2 * LANE]
    s2 = tab[:, base + 2 * LANE:base + 3 * LANE]
    n = x.shape[1] // LANE
    return jnp.concatenate(
        [_rope_slab(x[:, j * LANE:(j + 1) * LANE], c, s1, s2, half) for j in range(n)], axis=1)


def _split3(x):
    hi = x.astype(BF16)
    r1 = x - hi.astype(F32)
    mid = r1.astype(BF16)
    lo = (r1 - mid.astype(F32)).astype(BF16)
    return hi, mid, lo


_O_Q, _O_CKV, _O_FQ, _O_FK, _O_FV, _O_KR, _O_F, _A_COLS = 0, 512, 768, 1280, 1792, 2304, 2432, 2560


def _store_v(v_o, v, vt):
    if not vt:
        v_o[...] = v.astype(BF16)
        return
    for p in range(v.shape[1] // LANE):
        v_o[p, 0] = v[:, p * LANE:(p + 1) * LANE].T.astype(BF16)


def _v_out(n, width, tm, vt):
    if not vt:
        return pl.BlockSpec((tm, width), lambda i: (i, 0)), jax.ShapeDtypeStruct((n, width), BF16)
    p = width // LANE
    return (pl.BlockSpec((p, 1, LANE, tm), lambda i: (0, i, 0, 0)),
            jax.ShapeDtypeStruct((p, n // tm, LANE, tm), BF16))


def _proj_a_kernel(x_ref, tab_ref, gmix_ref, win_ref, gq_ref, wuq_ref, gkv_ref, wkv_ref, bf_ref,
                   ckv_o, krope_o, fk_o, fv_o, logf_o, lfs_o, qm_o, km_o, vm_o, fq_o, fkb_o, fvb_o,
                   *, fox_scale, vt):
    h = _rms(x_ref[...], gmix_ref[...], 1e-6).astype(BF16)
    tab = tab_ref[...]
    ql = _dot(h, win_ref[:, _O_Q:_O_Q + Q_LORA])
    qn = _rms(ql, gq_ref[...], 1e-6).astype(BF16)
    q = _dot(qn, wuq_ref[...])
    qm_o[...] = _rope_wide(q, tab, 0, MLA_ROPE // 2).astype(BF16)
    cn = _rms(_dot(h, win_ref[:, _O_CKV:_O_CKV + KV_LORA]), gkv_ref[...], 1e-6)
    ckv_o[...] = cn
    kr = _rope_wide(_dot(h, win_ref[:, _O_KR:_O_KR + LANE]), tab, 3 * LANE, MLA_ROPE // 2)
    krope_o[...] = kr[:, :MLA_ROPE]
    kin = jnp.concatenate([cn.astype(BF16), kr.astype(BF16)], axis=1)
    kv = _dot(kin, wkv_ref[...])
    km_o[...] = kv[:, :H_MLA * LANE].astype(BF16)
    _store_v(vm_o, kv[:, H_MLA * LANE:], vt)
    fq_o[...] = (_dot(h, win_ref[:, _O_FQ:_O_FQ + 512]) * fox_scale).astype(BF16)
    fk = _dot(h, win_ref[:, _O_FK:_O_FK + 512])
    fkb_o[...] = fk.astype(BF16)
    fv = _dot(h, win_ref[:, _O_FV:_O_FV + 512])
    _store_v(fvb_o, fv, vt)
    fk_o[...] = pltpu.einshape("m(hd)->mhd", fk, h=H_FOX)
    fv_o[...] = pltpu.einshape("m(hd)->mhd", fv, h=H_FOX)
    z = _dot(h, win_ref[:, _O_F:_O_F + LANE]) + bf_ref[...]
    ls = jnp.minimum(z, 0.0) - jnp.log1p(jnp.exp(-jnp.abs(z)))
    lane = lax.broadcasted_iota(jnp.int32, ls.shape, 1)
    ls = jnp.where(lane < H_FOX, ls, 0.0)
    lfs_o[...] = ls
    logf_o[...] = ls[:, :H_FOX]


def _proj_a(x, tab, tab_period, gmix, win_p, gq, wuq_p, gkv, wkv_p, bf_slab, tm, vt):
    n, d = x.shape
    nt = n // tm
    nper = tab_period // tm
    row = lambda w: pl.BlockSpec((tm, w), lambda i: (i, 0))
    plain = lambda w, t: (row(w), jax.ShapeDtypeStruct((n, w), t))
    heads = lambda: (pl.BlockSpec((tm, H_FOX, FOX_DH), lambda i: (i, 0, 0)),
                     jax.ShapeDtypeStruct((n, H_FOX, FOX_DH), F32))
    outs = [plain(KV_LORA, F32), plain(MLA_ROPE, F32), heads(), heads(),
            plain(H_FOX, F32), plain(LANE, F32),
            plain(H_MLA * LANE, BF16), plain(H_MLA * LANE, BF16), _v_out(n, H_MLA * MLA_V, tm, vt),
            plain(512, BF16), plain(512, BF16), _v_out(n, 512, tm, vt)]
    return pl.pallas_call(
        functools.partial(_proj_a_kernel, fox_scale=FOX_DH ** -0.5 * LOG2E, vt=vt),
        grid=(nt,),
        in_specs=[row(d),
                  pl.BlockSpec((tm, 6 * LANE), lambda i: (i % nper, 0)),
                  _const_spec(gmix.shape), _const_spec(win_p.shape), _const_spec(gq.shape),
                  _const_spec(wuq_p.shape), _const_spec(gkv.shape), _const_spec(wkv_p.shape),
                  _const_spec(bf_slab.shape)],
        out_specs=[o[0] for o in outs],
        out_shape=[o[1] for o in outs],
        compiler_params=_cparams("parallel"),
        name="proj_a",
    )(x, tab, gmix, win_p, gq, wuq_p, gkv, wkv_p, bf_slab)


def _with_aug(x, aug):
    x = x.astype(F32)
    first = lax.broadcasted_iota(jnp.int32, (x.shape[0], LANE), 1) < LANE // 2
    out = []
    for p in range(H_FOX // 2):
        pair = x[:, p * LANE:(p + 1) * LANE]
        out.append(jnp.where(first, pair, aug[:, 2 * p * LANE:(2 * p + 1) * LANE]))
        out.append(jnp.where(first, pltpu.roll(pair, LANE // 2, 1),
                             aug[:, (2 * p + 1) * LANE:(2 * p + 2) * LANE]))
    return jnp.concatenate(out, axis=1).astype(BF16)


def _cum_kernel(*refs, with_q):
    lf_ref, init_ref, tri_ref, ek_ref, eq_ref, onek_ref, oneq_ref, fk_ref = refs[:8]
    fq_ref = refs[8] if with_q else None
    outs = refs[9 if with_q else 8:-1]
    carry = refs[-1]

    @pl.when(pl.program_id(1) == 0)
    def _():
        carry[...] = init_ref[0]

    tri = tri_ref[...]
    hi, mid, lo = _split3(lf_ref[0])
    cum = carry[...] + (_dot(tri, hi) + _dot(tri, mid) + _dot(tri, lo))
    t = cum.shape[0]
    carry[...] = cum[t - 1:t, :]
    outs[0][0] = cum
    hi, mid, lo = _split3(cum * LOG2E)
    kaug = _dot(hi, ek_ref[0]) + _dot(mid, ek_ref[1]) + _dot(lo, ek_ref[2]) + onek_ref[...]
    outs[1][...] = _with_aug(fk_ref[...], kaug)
    if with_q:
        qaug = _dot(hi, eq_ref[0]) + _dot(mid, eq_ref[1]) + _dot(lo, eq_ref[2]) + oneq_ref[...]
        outs[2][...] = _with_aug(fq_ref[...], qaug)


def _aug_tables():
    w = H_FOX * LANE
    ek = [[[0.0] * w for _ in range(LANE)] for _ in range(3)]
    eq = [[[0.0] * w for _ in range(LANE)] for _ in range(3)]
    onek = [0.0] * w
    oneq = [0.0] * w
    for hh in range(H_FOX):
        base = hh * LANE + FOX_DH
        for s in range(3):
            ek[s][hh][base + s] = -1.0
            eq[s][hh][base + 3 + s] = 1.0
            onek[base + 3 + s] = 1.0
            oneq[base + s] = 1.0
    return (jnp.array(ek, BF16), jnp.array(eq, BF16),
            jnp.array([onek], F32), jnp.array([oneq], F32))


def _cum_aug(lf_slab, init, fk, fq, t):
    b, l, _ = lf_slab.shape
    nt = l // t
    tri = (lax.broadcasted_iota(jnp.int32, (t, t), 1)
           <= lax.broadcasted_iota(jnp.int32, (t, t), 0)).astype(BF16)
    consts = (tri,) + _aug_tables()
    w = H_FOX * LANE
    rows = lambda width: pl.BlockSpec((t, width), lambda i, j: (i * nt + j, 0))
    xs = [fk] + ([fq] if fq is not None else [])
    return pl.pallas_call(
        functools.partial(_cum_kernel, with_q=fq is not None),
        grid=(b, nt),
        in_specs=([pl.BlockSpec((1, t, LANE), lambda i, j: (i, j, 0)),
                   pl.BlockSpec((1, 1, LANE), lambda i, j: (i, 0, 0))]
                  + [_const_spec(c.shape) for c in consts] + [rows(x.shape[1]) for x in xs]),
        out_specs=[pl.BlockSpec((1, t, LANE), lambda i, j: (i, j, 0))] + [rows(w) for _ in xs],
        out_shape=([jax.ShapeDtypeStruct((b, l, LANE), F32)]
                   + [jax.ShapeDtypeStruct((b * l, w), BF16) for _ in xs]),
        scratch_shapes=[pltpu.VMEM((1, LANE), F32)],
        compiler_params=_cparams("arbitrary", "arbitrary"),
        name="cum_aug",
    )(lf_slab, init, *consts, *xs)


def _softmax_step(qs, ks, v, mask, m_ref, l_ref, acc_ref):
    tk = v.shape[0]
    for j in range(2):
        s = lax.dot_general(qs[j], ks[j], (((1,), (1,)), ((), ())), preferred_element_type=F32)
        if mask is not None:
            s = jnp.where(mask, s, NEG)
        m_prev = m_ref[j]
        m_next = jnp.maximum(m_prev, jnp.max(s, axis=1, keepdims=True))
        p = jnp.exp2(s - jnp.concatenate([m_next] * (tk // LANE), axis=1)) if tk >= LANE \
            else jnp.exp2(s - m_next[:, :tk])
        alpha = jnp.exp2(m_prev - m_next)
        l_ref[j] = alpha * l_ref[j] + jnp.sum(p, axis=1, keepdims=True)
        m_ref[j] = m_next
        acc_ref[j] = acc_ref[j] * alpha + _dot(p.astype(BF16), v)


def _init_stats(m_ref, l_ref, acc_ref):
    m_ref[...] = jnp.full(m_ref.shape, NEG, F32)
    l_ref[...] = jnp.zeros(l_ref.shape, F32)
    acc_ref[...] = jnp.zeros(acc_ref.shape, F32)


def _half_masks(shape):
    lane = lax.broadcasted_iota(jnp.int32, shape, 1)
    return (lane % LANE) < (LANE // 2)


def _queries(q):
    return [q[:, :LANE], q[:, LANE:]]


def _keys(mode, k):
    return [k, k] if mode == "diff" else [k[:, :LANE], k[:, LANE:]]


def _visible(mode, q_pos, k_pos):
    if mode == "fox":
        return k_pos <= q_pos
    return (k_pos // CHUNK) <= (q_pos // CHUNK)


def _finish(mode, l_ref, acc_ref, extra, lam_init):
    o0 = acc_ref[0] / l_ref[0]
    o1 = acc_ref[1] / l_ref[1]
    if mode != "diff":
        return jnp.where(_half_masks(o0.shape), o0, o1)
    lam_ref, gsub_ref = extra
    lp = lam_ref[...]
    lam = (jnp.exp(jnp.sum(lp[0:1] * lp[1:2], axis=1, keepdims=True))
           - jnp.exp(jnp.sum(lp[2:3] * lp[3:4], axis=1, keepdims=True)) + lam_init)
    o = o0 - lam * o1
    return _rms(o, gsub_ref[...], 1e-5) * (1.0 - lam_init)


def _logits_t(qs, ks):
    return tuple(lax.dot_general(ks[j], qs[j], (((1,), (1,)), ((), ())), preferred_element_type=F32)
                 for j in range(2))


ONES_ROWS = 16


def _softmax_step_t(st, mx, vt1, mask, m_ref, acc_ref):
    if mask is not None:
        st = jnp.where(mask, st, NEG)
        mx = jnp.max(st, axis=0, keepdims=True)
    m_prev = m_ref[...]
    m_next = jnp.maximum(m_prev, mx)
    pt = jnp.exp2(st - m_next)
    alpha = jnp.exp2(m_prev - m_next)
    m_ref[...] = m_next
    acc_ref[...] = acc_ref[...] * alpha + _dot(vt1, pt.astype(BF16))


def _finish_t(mode, acc_ref, extra, lam_init):
    rows = acc_ref.shape[1] - ONES_ROWS
    o0 = acc_ref[0, :rows] / acc_ref[0, rows:rows + 1]
    o1 = acc_ref[1, :rows] / acc_ref[1, rows:rows + 1]
    if mode != "diff":
        return jnp.concatenate([o0, o1], axis=0).T
    lam_ref, gsub_ref = extra
    lp = lam_ref[...]
    lam = (jnp.exp(jnp.sum(lp[0:1] * lp[1:2], axis=1, keepdims=True))
           - jnp.exp(jnp.sum(lp[2:3] * lp[3:4], axis=1, keepdims=True)) + lam_init)
    o = (o0 - lam * o1).T
    return _rms(o, gsub_ref[...], 1e-5) * (1.0 - lam_init)


def _flash_kernel(q_ref, k_ref, vt_ref, *rest, mode, nq, t, lam_init):
    n_extra = 2 if mode == "diff" else 0
    extra = rest[:n_extra]
    o_ref = rest[n_extra]
    m_ref, acc_ref, st_ref, mx_ref = rest[n_extra + 1:]
    half = LANE // 2
    rows_of = lambda blk: pl.ds(pl.multiple_of(blk * t, t), t)

    def produce(task, slot, j):
        qi, kb = task
        q = _queries(q_ref[rows_of(qi), :])[j]
        k = _keys(mode, k_ref[rows_of(kb), :])[j]
        st = lax.dot_general(k, q, (((1,), (1,)), ((), ())), preferred_element_type=F32)
        st_ref[slot, j] = st
        mx_ref[slot, j] = jnp.max(st, axis=0, keepdims=True)

    def fold(kb, slot, j, diagonal):
        vt = vt_ref[kb]
        if mode != "diff":
            vt = vt[j * half:(j + 1) * half]
        vt1 = jnp.concatenate([vt, jnp.ones((ONES_ROWS, t), BF16)], axis=0)
        msk = _visible(mode, lax.broadcasted_iota(jnp.int32, (t, t), 1),
                       lax.broadcasted_iota(jnp.int32, (t, t), 0)) if diagonal else None
        _softmax_step_t(st_ref[slot, j], mx_ref[slot, j], vt1, msk, m_ref.at[j], acc_ref.at[j])

    def both(cur, slot, nxt, diagonal):
        for j in range(2):
            produce(nxt, 1 - slot, j)
            fold(cur[1], slot, j, diagonal)

    def stage(cur, slot, nxt):
        qi, kb = cur

        @pl.when(kb == 0)
        def _():
            m_ref[...] = jnp.full(m_ref.shape, NEG, F32)
            acc_ref[...] = jnp.zeros(acc_ref.shape, F32)

        @pl.when(kb != qi)
        def _():
            both(cur, slot, nxt, False)

        @pl.when(kb == qi)
        def _():
            both(cur, slot, nxt, True)
            o_ref[rows_of(qi), :] = _finish_t(mode, acc_ref, extra, lam_init).astype(o_ref.dtype)

    def succ(task):
        qi, kb = task
        diag = kb == qi
        nqi = jnp.where(diag, qi + 1, qi)
        over = nqi >= nq
        return jnp.where(over, 0, nqi), jnp.where(diag | over, 0, kb + 1)

    first = (jnp.int32(0), jnp.int32(0))
    for j in range(2):
        produce(first, 0, j)

    def pair(_, task):
        nxt = succ(task)
        nxt2 = succ(nxt)
        stage(task, 0, nxt)
        stage(nxt, 1, nxt2)
        return nxt2

    n_tasks = nq * (nq + 1) // 2
    last = lax.fori_loop(0, n_tasks // 2, pair, first)
    if n_tasks % 2:
        stage(last, 0, first)


def _flash(mode, q, k, vt, extra, b, s, t, lam_init=0.0):
    p = vt.shape[0]
    wq, wk = q.shape[1] // p, k.shape[1] // p
    rows = LANE if mode == "diff" else LANE // 2
    seq = lambda w: pl.BlockSpec((s, w), lambda bi, pi: (bi, pi))
    return pl.pallas_call(
        functools.partial(_flash_kernel, mode=mode, nq=s // t, t=t, lam_init=lam_init),
        grid=(b, p),
        in_specs=([seq(wq), seq(wk), pl.BlockSpec((None, s // t, LANE, t), lambda bi, pi: (pi, bi, 0, 0))]
                  + [_const_spec(e.shape) for e in extra]),
        out_specs=seq(LANE),
        out_shape=jax.ShapeDtypeStruct((b * s, p * LANE), BF16),
        scratch_shapes=[pltpu.VMEM((2, 1, t), F32),
                        pltpu.VMEM((2, rows + ONES_ROWS, t), F32),
                        pltpu.VMEM((2, 2, t, t), F32), pltpu.VMEM((2, 2, 1, t), F32)],
        compiler_params=_cparams("parallel", "parallel"),
        name="flash_" + mode,
    )(q, k, vt, *extra)


def _sample_kernel(q_ref, kp_ref, kn_ref, vp_ref, vn_ref, *rest, mode, past, tq, tkp, lam_init):
    n_extra = 2 if mode == "diff" else 0
    extra = rest[:n_extra]
    o_ref = rest[n_extra]
    m_ref, l_ref, acc_ref = rest[n_extra + 1:]
    _init_stats(m_ref, l_ref, acc_ref)
    new = lambda r: (r[0] if len(r.shape) == 3 else r[...])[:tq]
    qs = _queries(new(q_ref))
    for c in range(past // tkp):
        sl = slice(c * tkp, (c + 1) * tkp)
        ks = _keys(mode, kp_ref[0, sl, :].astype(BF16))
        _softmax_step(qs, ks, vp_ref[0, sl, :].astype(BF16), None, m_ref, l_ref, acc_ref)
    pad = lambda a: jnp.concatenate([a, jnp.zeros((LANE - tq, a.shape[1]), a.dtype)], axis=0)
    q_pos = past + lax.broadcasted_iota(jnp.int32, (tq, LANE), 0)
    k_idx = lax.broadcasted_iota(jnp.int32, (tq, LANE), 1)
    mask = _visible(mode, q_pos, past + k_idx) & (k_idx < tq)
    _softmax_step(qs, _keys(mode, pad(new(kn_ref))), pad(vn_ref[...]), mask, m_ref, l_ref, acc_ref)
    o_ref[...] = _finish(mode, l_ref, acc_ref, extra, lam_init).astype(o_ref.dtype)


def _sample_attn(mode, q, kp, kn, vp, vn, extra, bs, tq, past, lam_init=0.0):
    p = vn.shape[1] // LANE
    tkp = min(past, 1024)

    def new_spec(a):
        w = a.shape[-1] // p
        if a.ndim == 3:
            return pl.BlockSpec((1, a.shape[1], w), lambda bi, pi: (bi, 0, pi))
        return pl.BlockSpec((tq, w), lambda bi, pi: (bi, pi))

    def past_spec(a):
        return pl.BlockSpec((1, past, a.shape[-1] // p), lambda bi, pi: (bi, 0, pi))

    return pl.pallas_call(
        functools.partial(_sample_kernel, mode=mode, past=past, tq=tq, tkp=tkp, lam_init=lam_init),
        grid=(bs, p),
        in_specs=([new_spec(q), past_spec(kp), new_spec(kn), past_spec(vp), new_spec(vn)]
                  + [_const_spec(e.shape) for e in extra]),
        out_specs=pl.BlockSpec((tq, LANE), lambda bi, pi: (bi, pi)),
        out_shape=jax.ShapeDtypeStruct((bs * tq, p * LANE), BF16),
        scratch_shapes=[pltpu.VMEM((2, tq, LANE), F32)] * 3,
        compiler_params=_cparams("parallel", "parallel"),
        name="sample_" + mode,
    )(q, kp, kn, vp, vn, *extra)


def _kv_up_kernel(ckv_ref, kr_ref, wkv_ref, k_o, v_o):
    kin = jnp.concatenate([ckv_ref[...].astype(BF16), kr_ref[...].astype(BF16)], axis=1)
    kv = _dot(kin, wkv_ref[...])
    k_o[...] = kv[:, :H_MLA * LANE].astype(BF16)
    v_o[...] = kv[:, H_MLA * LANE:].astype(BF16)


def _kv_up(ckv, kr_slab, wkv_p, tm):
    n = ckv.shape[0]
    return pl.pallas_call(
        _kv_up_kernel,
        grid=(n // tm,),
        in_specs=[pl.BlockSpec((tm, KV_LORA), lambda i: (i, 0)),
                  pl.BlockSpec((tm, LANE), lambda i: (i, 0)),
                  _const_spec(wkv_p.shape)],
        out_specs=[pl.BlockSpec((tm, H_MLA * LANE), lambda i: (i, 0)),
                   pl.BlockSpec((tm, H_MLA * MLA_V), lambda i: (i, 0))],
        out_shape=[jax.ShapeDtypeStruct((n, H_MLA * LANE), BF16),
                   jax.ShapeDtypeStruct((n, H_MLA * MLA_V), BF16)],
        compiler_params=_cparams("parallel"),
        name="kv_up",
    )(ckv, kr_slab, wkv_p)


def _mlp_kernel(*refs, n_o, final, ff_chunk):
    o_refs = refs[:n_o]
    x_ref, wout_ref, gmlp_ref, wup_ref, wdown_ref = refs[n_o:n_o + 5]
    gfin_ref = refs[n_o + 5] if final else None
    out_ref = refs[-1]
    proj = None
    r0 = 0
    for o_ref in o_refs:
        w = o_ref.shape[1]
        t = _dot(o_ref[...], wout_ref[r0:r0 + w, :])
        proj = t if proj is None else proj + t
        r0 += w
    x1 = x_ref[...] + proj
    h = _rms(x1, gmlp_ref[...], 1e-6).astype(BF16)
    down = None
    for c in range(wup_ref.shape[1] // ff_chunk):
        u = _dot(h, wup_ref[:, c * ff_chunk:(c + 1) * ff_chunk])
        a = jnp.square(jnp.maximum(u, 0.0)).astype(BF16)
        t = _dot(a, wdown_ref[c * ff_chunk:(c + 1) * ff_chunk, :])
        down = t if down is None else down + t
    x2 = x1 + down
    out_ref[...] = _rms(x2, gfin_ref[...], 1e-6) if final else x2


def _outproj_mlp(os_, x, wout, gmlp, wup, wdown, gfin, tm):
    n, d = x.shape
    final = gfin is not None
    consts = [wout, gmlp, wup, wdown] + ([gfin] if final else [])
    return pl.pallas_call(
        functools.partial(_mlp_kernel, n_o=len(os_), final=final, ff_chunk=1024),
        grid=(n // tm,),
        in_specs=([pl.BlockSpec((tm, o.shape[1]), lambda i: (i, 0)) for o in os_]
                  + [pl.BlockSpec((tm, d), lambda i: (i, 0))]
                  + [_const_spec(c.shape) for c in consts]),
        out_specs=pl.BlockSpec((tm, d), lambda i: (i, 0)),
        out_shape=jax.ShapeDtypeStruct((n, d), F32),
        compiler_params=_cparams("parallel"),
        name="outproj_mlp",
    )(*os_, x, *consts)


def _proj_c_kernel(x_ref, tab_ref, gmix_ref, w_ref, k_o, v_o, qb_o, kb_o, vb_o, *, vt):
    h = _rms(x_ref[...], gmix_ref[...], 1e-6).astype(BF16)
    tab = tab_ref[...]
    cw = H_DIFF * 2 * DIFF_DH
    q = _rope_wide(_dot(h, w_ref[:, :cw]), tab, 0, DIFF_ROT // 2)
    first = lax.broadcasted_iota(jnp.int32, (q.shape[0], LANE), 1) < DIFF_DH
    slabs = []
    for hh in range(H_DIFF):
        qh = q[:, hh * LANE:(hh + 1) * LANE]
        slabs += [jnp.where(first, qh, 0.0), jnp.where(first, 0.0, qh)]
    qb_o[...] = jnp.concatenate(slabs, axis=1).astype(BF16)
    k = _rope_wide(_dot(h, w_ref[:, cw:2 * cw]), tab, 3 * LANE, DIFF_ROT // 2)
    kb_o[...] = k.astype(BF16)
    v = _dot(h, w_ref[:, 2 * cw:])
    _store_v(vb_o, v, vt)
    k_o[...] = pltpu.einshape("m(hd)->mhd", k, h=H_DIFF)
    v_o[...] = pltpu.einshape("m(hd)->mhd", v, h=H_DIFF)


def _proj_c(x, tab, tab_period, gmix, w, tm, vt):
    n, d = x.shape
    nper = tab_period // tm
    cw = H_DIFF * 2 * DIFF_DH
    plain = lambda t, w=cw: (pl.BlockSpec((tm, w), lambda i: (i, 0)), jax.ShapeDtypeStruct((n, w), t))
    heads = lambda: (pl.BlockSpec((tm, H_DIFF, LANE), lambda i: (i, 0, 0)),
                     jax.ShapeDtypeStruct((n, H_DIFF, LANE), F32))
    outs = [heads(), heads(), plain(BF16, 2 * cw), plain(BF16), _v_out(n, cw, tm, vt)]
    return pl.pallas_call(
        functools.partial(_proj_c_kernel, vt=vt),
        grid=(n // tm,),
        in_specs=[pl.BlockSpec((tm, d), lambda i: (i, 0)),
                  pl.BlockSpec((tm, 6 * LANE), lambda i: (i % nper, 0)),
                  _const_spec(gmix.shape), _const_spec(w.shape)],
        out_specs=[o[0] for o in outs],
        out_shape=[o[1] for o in outs],
        compiler_params=_cparams("parallel"),
        name="proj_c",
    )(x, tab, gmix, w)


def _angles(pos, half):
    inv = ROPE_THETA ** (-jnp.arange(half, dtype=F32) / half)
    ang = pos.astype(F32)[:, None] * inv[None, :]
    return jnp.cos(ang), jnp.sin(ang)


def _tables_a(pos):
    cos, sin = _angles(pos, MLA_ROPE // 2)
    t = pos.shape[0]
    z = lambda w: jnp.zeros((t, w), F32)
    sc = (MLA_NOPE + MLA_ROPE) ** -0.5 * LOG2E
    cq = sc * jnp.concatenate([jnp.ones((t, MLA_NOPE), F32), cos, cos, z(32)], axis=1)
    s1q = sc * jnp.concatenate([z(MLA_NOPE + 16), sin, z(32)], axis=1)
    s2q = sc * jnp.concatenate([z(MLA_NOPE), -sin, z(48)], axis=1)
    ck = jnp.concatenate([cos, cos, z(96)], axis=1)
    s1k = jnp.concatenate([z(16), sin, z(96)], axis=1)
    s2k = jnp.concatenate([-sin, z(112)], axis=1)
    return jnp.concatenate([cq, s1q, s2q, ck, s1k, s2k], axis=1)


def _tables_c(pos):
    cos, sin = _angles(pos, DIFF_ROT // 2)
    t = pos.shape[0]
    z = lambda w: jnp.zeros((t, w), F32)
    two = lambda a: jnp.concatenate([a, a], axis=1)
    c = two(jnp.concatenate([cos, cos, jnp.ones((t, 48), F32)], axis=1))
    s1 = two(jnp.concatenate([z(8), sin, z(48)], axis=1))
    s2 = two(jnp.concatenate([-sin, z(56)], axis=1))
    sc = DIFF_DH ** -0.5 * LOG2E
    return jnp.concatenate([sc * c, sc * s1, sc * s2, c, s1, s2], axis=1)


def _pad_cols(a, w):
    return jnp.pad(a, ((0, 0), (0, w - a.shape[1])))


def _prep_a(w_in, w_uq, w_ukv, b_f):
    o1, o2, o3 = Q_LORA, Q_LORA + KV_LORA, Q_LORA + KV_LORA + MLA_ROPE
    o4, o5, o6 = o3 + 512, o3 + 1024, o3 + 1536
    win_p = jnp.concatenate(
        [w_in[:, :o2], w_in[:, o3:o6], _pad_cols(w_in[:, o2:o3], LANE), _pad_cols(w_in[:, o6:], LANE)],
        axis=1).astype(BF16)
    dq = MLA_NOPE + MLA_ROPE
    wuq_p = jnp.pad(w_uq.reshape(Q_LORA, H_MLA, dq), ((0, 0), (0, 0), (0, LANE - dq)))
    wuq_p = wuq_p.reshape(Q_LORA, H_MLA * LANE).astype(BF16)
    wkv3 = w_ukv.reshape(KV_LORA, H_MLA, MLA_NOPE + MLA_V)
    wk = jnp.pad(wkv3[:, :, :MLA_NOPE], ((0, 0), (0, 0), (0, LANE - MLA_NOPE))).reshape(KV_LORA, H_MLA * LANE)
    wv = wkv3[:, :, MLA_NOPE:].reshape(KV_LORA, H_MLA * MLA_V)
    place = jnp.pad(jnp.eye(MLA_ROPE, dtype=F32), ((0, LANE - MLA_ROPE), (MLA_NOPE, LANE - MLA_NOPE - MLA_ROPE)))
    place = jnp.concatenate([jnp.tile(place, (1, H_MLA)), jnp.zeros((LANE, H_MLA * MLA_V), F32)], axis=1)
    wkv_p = jnp.concatenate([jnp.concatenate([wk, wv], axis=1), place], axis=0).astype(BF16)
    bf_slab = _pad_cols(b_f[None, :], LANE)
    return win_p, wuq_p, wkv_p, bf_slab


def _row_tile(n, want):
    t = min(n, want)
    while n % t:
        t //= 2
    return t


def kernel(x_prompt, x_sample, cache_mla_ckv, cache_mla_krope, cache_fox_k, cache_fox_v, cache_fox_logf,
           cache_diff_k, cache_diff_v, g_mix, a_w_in, a_g_q, a_w_uq, a_g_kv, a_w_ukv, a_b_f, a_w_out,
           c_w_in, c_lam, c_g_sub, c_w_out, g_mlp, w_up, w_down, g_final):
    b, s, d = x_prompt.shape
    bs, ts, _ = x_sample.shape
    past = cache_mla_ckv.shape[2]
    n, ns = b * s, bs * ts
    tm = _row_tile(s, 512)
    tq = tk = tm
    tms = _row_tile(ns, 512)
    lam_init = 0.8 - 0.6 * math.exp(-0.3 * 1)
    row = lambda a: a[None, :]

    xp = x_prompt.reshape(n, d)
    xs = x_sample.reshape(ns, d)
    pos_p = jnp.arange(s)
    pos_s = past + jnp.arange(ts)

    win_p, wuq_p, wkv_p, bf_slab = _prep_a(a_w_in[0], a_w_uq[0], a_w_ukv[0], a_b_f[0])
    wa = (row(g_mix[0]), win_p, row(a_g_q[0]), wuq_p, row(a_g_kv[0]), wkv_p, bf_slab)
    tab_ap = _tables_a(pos_p)
    tab_as = jnp.tile(_tables_a(pos_s), (tms // ts, 1))
    (ckv_p, krope_p, fk_p, fv_p, logf_p, lfs_p, qm_p, km_p, vm_p, fq_p, fkb_p, fvb_p) = _proj_a(
        xp, tab_ap, s, *wa, tm, True)
    (ckv_s, krope_s, fk_s, fv_s, logf_s, lfs_s, qm_s, km_s, vm_s, fq_s, fkb_s, fvb_s) = _proj_a(
        xs, tab_as, tms, *wa, tms, False)

    _, fks_p, fqs_p = _cum_aug(lfs_p.reshape(b, s, LANE), jnp.zeros((b, 1, LANE), F32), fkb_p, fq_p, tm)
    om_p = _flash("mla", qm_p, km_p, vm_p, [], b, s, tq)
    of_p = _flash("fox", fqs_p, fks_p, fvb_p, [], b, s, tq)

    lf_past = jnp.pad(cache_fox_logf[0], ((0, 0), (0, 0), (0, LANE - H_FOX)))
    cum_past, fks_past = _cum_aug(lf_past, jnp.zeros((bs, 1, LANE), F32),
                                  cache_fox_k[0].reshape(bs * past, -1), None, _row_tile(past, 512))
    rpad = 128
    pad_rows = lambda a: jnp.pad(a.reshape(bs, ts, -1), ((0, 0), (0, rpad - ts), (0, 0)))
    _, fks_new, fqs_new = _cum_aug(pad_rows(lfs_s), cum_past[:, past - 1:past, :],
                                   pad_rows(fkb_s).reshape(bs * rpad, -1),
                                   pad_rows(fq_s).reshape(bs * rpad, -1), rpad)
    kr_past = jnp.pad(cache_mla_krope[0].reshape(bs * past, MLA_ROPE), ((0, 0), (0, LANE - MLA_ROPE)))
    km_past, vm_past = _kv_up(cache_mla_ckv[0].reshape(bs * past, KV_LORA), kr_past, wkv_p,
                              _row_tile(bs * past, 1024))
    om_s = _sample_attn("mla", qm_s, km_past.reshape(bs, past, -1), km_s,
                        vm_past.reshape(bs, past, -1), vm_s, [], bs, ts, past)
    of_s = _sample_attn("fox", fqs_new.reshape(bs, rpad, -1), fks_past.reshape(bs, past, -1),
                        fks_new.reshape(bs, rpad, -1), cache_fox_v[0].reshape(bs, past, -1), fvb_s,
                        [], bs, ts, past)

    wout_a = a_w_out[0].astype(BF16)
    wup0, wdown0 = w_up[0].astype(BF16), w_down[0].astype(BF16)
    x1_p = _outproj_mlp([om_p, of_p], xp, wout_a, row(g_mlp[0]), wup0, wdown0, None, tm)
    x1_s = _outproj_mlp([om_s, of_s], xs, wout_a, row(g_mlp[0]), wup0, wdown0, None, tms)

    wc = c_w_in[0].astype(BF16)
    tab_cp = _tables_c(pos_p)
    tab_cs = jnp.tile(_tables_c(pos_s), (tms // ts, 1))
    dk_p, dv_p, dqb_p, dkb_p, dvb_p = _proj_c(x1_p, tab_cp, s, row(g_mix[1]), wc, tm, True)
    dk_s, dv_s, dqb_s, dkb_s, dvb_s = _proj_c(x1_s, tab_cs, tms, row(g_mix[1]), wc, tms, False)
    extra = [c_lam[0], row(c_g_sub[0])]
    oc_p = _flash("diff", dqb_p, dkb_p, dvb_p, extra, b, s, tq, lam_init)
    oc_s = _sample_attn("diff", dqb_s, cache_diff_k[0].reshape(bs, past, -1), dkb_s,
                        cache_diff_v[0].reshape(bs, past, -1), dvb_s, extra, bs, ts, past, lam_init)
    wout_c = c_w_out[0].astype(BF16)
    wup1, wdown1 = w_up[1].astype(BF16), w_down[1].astype(BF16)
    y_p = _outproj_mlp([oc_p], x1_p, wout_c, row(g_mlp[1]), wup1, wdown1, row(g_final), tm)
    y_s = _outproj_mlp([oc_s], x1_s, wout_c, row(g_mlp[1]), wup1, wdown1, row(g_final), tms)

    def pack(bb, tt, ckv, krope, fk, fv, logf, dk, dv):
        return (ckv.reshape(1, bb, tt, KV_LORA), krope.reshape(1, bb, tt, MLA_ROPE),
                fk.reshape(1, bb, tt, H_FOX, FOX_DH), fv.reshape(1, bb, tt, H_FOX, FOX_DH),
                logf.reshape(1, bb, tt, H_FOX),
                dk.reshape(1, bb, tt, H_DIFF, 2 * DIFF_DH), dv.reshape(1, bb, tt, H_DIFF, 2 * DIFF_DH))

    return ((y_p.reshape(b, s, d), y_s.reshape(bs, ts, d))
            + pack(b, s, ckv_p, krope_p, fk_p, fv_p, logf_p, dk_p, dv_p)
            + pack(bs, ts, ckv_s, krope_s, fk_s, fv_s, logf_s, dk_s, dv_s))
```

```python
import functools
import math

import jax
import jax.numpy as jnp
from jax import lax
from jax.experimental import pallas as pl
from jax.experimental.pallas import tpu as pltpu

F32 = jnp.float32
BF16 = jnp.bfloat16

CHUNK = 64
ROPE_THETA = 500000.0
H_MLA, MLA_NOPE, MLA_ROPE, MLA_V = 8, 64, 32, 64
Q_LORA, KV_LORA = 512, 256
H_FOX, FOX_DH = 8, 64
H_DIFF, DIFF_DH = 8, 64
DIFF_ROT = DIFF_DH // 4
LANE = 128
LOG2E = math.log2(math.e)
NEG = -1e30
VMEM_LIMIT = 56 * 1024 * 1024


def _cparams(*sem):
    return pltpu.CompilerParams(dimension_semantics=sem, vmem_limit_bytes=VMEM_LIMIT)


def _const_spec(shape):
    nd = len(shape)
    return pl.BlockSpec(shape, lambda *_: (0,) * nd, pipeline_mode=pl.Buffered(1))


def _rms(x, g, eps):
    y = x * lax.rsqrt(jnp.mean(x * x, axis=-1, keepdims=True) + eps)
    return y * g


def _dot(a, b):
    return jnp.dot(a, b, preferred_element_type=F32)


def _rope_slab(x, c, s1, s2, half):
    return x * c + pltpu.roll(x, half, 1) * s1 + pltpu.roll(x, LANE - half, 1) * s2


def _rope_wide(x, tab, base, half):
    c = tab[:, base:base + LANE]
    s1 = tab[:, base + LANE:base + 2 * LANE]
    s2 = tab[:, base + 2 * LANE:base + 3 * LANE]
    n = x.shape[1] // LANE
    return jnp.concatenate(
        [_rope_slab(x[:, j * LANE:(j + 1) * LANE], c, s1, s2, half) for j in range(n)], axis=1)


def _split3(x):
    hi = x.astype(BF16)
    r1 = x - hi.astype(F32)
    mid = r1.astype(BF16)
    lo = (r1 - mid.astype(F32)).astype(BF16)
    return hi, mid, lo


_O_Q, _O_CKV, _O_FQ, _O_FK, _O_FV, _O_KR, _O_F, _A_COLS = 0, 512, 768, 1280, 1792, 2304, 2432, 2560


def _store_v(v_o, v, vt):
    if not vt:
        v_o[...] = v.astype(BF16)
        return
    for p in range(v.shape[1] // LANE):
        v_o[p, 0] = v[:, p * LANE:(p + 1) * LANE].T.astype(BF16)


def _v_out(n, width, tm, vt, row_block=lambda i: i):
    if not vt:
        return (pl.BlockSpec((tm, width), lambda *g: (row_block(*g), 0)),
                jax.ShapeDtypeStruct((n, width), BF16))
    p = width // LANE
    return (pl.BlockSpec((p, 1, LANE, tm), lambda *g: (0, row_block(*g), 0, 0)),
            jax.ShapeDtypeStruct((p, n // tm, LANE, tm), BF16))


def _proj_a_kernel(x_ref, tab_ref, gmix_ref, win_ref, gq_ref, wuq_ref, gkv_ref, wkv_ref, bf_ref,
                   ckv_o, krope_o, fk_o, fv_o, logf_o, lfs_o, qm_o, km_o, vm_o, fq_o, fkb_o, fvb_o,
                   *, fox_scale, vt):
    h = _rms(x_ref[...], gmix_ref[...], 1e-6).astype(BF16)
    tab = tab_ref[...]
    ql = _dot(h, win_ref[:, _O_Q:_O_Q + Q_LORA])
    qn = _rms(ql, gq_ref[...], 1e-6).astype(BF16)
    q = _dot(qn, wuq_ref[...])
    _store_v(qm_o, _rope_wide(q, tab, 0, MLA_ROPE // 2), vt)
    cn = _rms(_dot(h, win_ref[:, _O_CKV:_O_CKV + KV_LORA]), gkv_ref[...], 1e-6)
    ckv_o[...] = cn
    kr = _rope_wide(_dot(h, win_ref[:, _O_KR:_O_KR + LANE]), tab, 3 * LANE, MLA_ROPE // 2)
    krope_o[...] = kr[:, :MLA_ROPE]
    kin = jnp.concatenate([cn.astype(BF16), kr.astype(BF16)], axis=1)
    kv = _dot(kin, wkv_ref[...])
    km_o[...] = kv[:, :H_MLA * LANE].astype(BF16)
    _store_v(vm_o, kv[:, H_MLA * LANE:], vt)
    fq_o[...] = (_dot(h, win_ref[:, _O_FQ:_O_FQ + 512]) * fox_scale).astype(BF16)
    fk = _dot(h, win_ref[:, _O_FK:_O_FK + 512])
    fkb_o[...] = fk.astype(BF16)
    fv = _dot(h, win_ref[:, _O_FV:_O_FV + 512])
    _store_v(fvb_o, fv, vt)
    fk_o[...] = pltpu.einshape("m(hd)->mhd", fk, h=H_FOX)
    fv_o[...] = pltpu.einshape("m(hd)->mhd", fv, h=H_FOX)
    z = _dot(h, win_ref[:, _O_F:_O_F + LANE]) + bf_ref[...]
    ls = jnp.minimum(z, 0.0) - jnp.log1p(jnp.exp(-jnp.abs(z)))
    lane = lax.broadcasted_iota(jnp.int32, ls.shape, 1)
    ls = jnp.where(lane < H_FOX, ls, 0.0)
    lfs_o[...] = ls
    logf_o[...] = ls[:, :H_FOX]


def _proj_a(x, tab, tab_period, gmix, win_p, gq, wuq_p, gkv, wkv_p, bf_slab, tm, vt):
    n, d = x.shape
    nt = n // tm
    nper = tab_period // tm
    row = lambda w: pl.BlockSpec((tm, w), lambda i: (i, 0))
    plain = lambda w, t: (row(w), jax.ShapeDtypeStruct((n, w), t))
    heads = lambda: (pl.BlockSpec((tm, H_FOX, FOX_DH), lambda i: (i, 0, 0)),
                     jax.ShapeDtypeStruct((n, H_FOX, FOX_DH), F32))
    outs = [plain(KV_LORA, F32), plain(MLA_ROPE, F32), heads(), heads(),
            plain(H_FOX, F32), plain(LANE, F32),
            _v_out(n, H_MLA * LANE, tm, vt), plain(H_MLA * LANE, BF16), _v_out(n, H_MLA * MLA_V, tm, vt),
            plain(512, BF16), plain(512, BF16), _v_out(n, 512, tm, vt)]
    return pl.pallas_call(
        functools.partial(_proj_a_kernel, fox_scale=FOX_DH ** -0.5 * LOG2E, vt=vt),
        grid=(nt,),
        in_specs=[row(d),
                  pl.BlockSpec((tm, 6 * LANE), lambda i: (i % nper, 0)),
                  _const_spec(gmix.shape), _const_spec(win_p.shape), _const_spec(gq.shape),
                  _const_spec(wuq_p.shape), _const_spec(gkv.shape), _const_spec(wkv_p.shape),
                  _const_spec(bf_slab.shape)],
        out_specs=[o[0] for o in outs],
        out_shape=[o[1] for o in outs],
        compiler_params=_cparams("parallel"),
        name="proj_a",
    )(x, tab, gmix, win_p, gq, wuq_p, gkv, wkv_p, bf_slab)


def _with_aug(x, aug):
    x = x.astype(F32)
    first = lax.broadcasted_iota(jnp.int32, (x.shape[0], LANE), 1) < LANE // 2
    out = []
    for p in range(H_FOX // 2):
        pair = x[:, p * LANE:(p + 1) * LANE]
        out.append(jnp.where(first, pair, aug[:, 2 * p * LANE:(2 * p + 1) * LANE]))
        out.append(jnp.where(first, pltpu.roll(pair, LANE // 2, 1),
                             aug[:, (2 * p + 1) * LANE:(2 * p + 2) * LANE]))
    return jnp.concatenate(out, axis=1)


def _cum_kernel(*refs, with_q, qt):
    lf_ref, init_ref, tri_ref, ek_ref, eq_ref, onek_ref, oneq_ref, fk_ref = refs[:8]
    fq_ref = refs[8] if with_q else None
    outs = refs[9 if with_q else 8:-1]
    carry = refs[-1]

    @pl.when(pl.program_id(1) == 0)
    def _():
        carry[...] = init_ref[0]

    tri = tri_ref[...]
    hi, mid, lo = _split3(lf_ref[0])
    cum = carry[...] + (_dot(tri, hi) + _dot(tri, mid) + _dot(tri, lo))
    t = cum.shape[0]
    carry[...] = cum[t - 1:t, :]
    outs[0][0] = cum
    hi, mid, lo = _split3(cum * LOG2E)
    kaug = _dot(hi, ek_ref[0]) + _dot(mid, ek_ref[1]) + _dot(lo, ek_ref[2]) + onek_ref[...]
    outs[1][...] = _with_aug(fk_ref[...], kaug).astype(BF16)
    if with_q:
        qaug = _dot(hi, eq_ref[0]) + _dot(mid, eq_ref[1]) + _dot(lo, eq_ref[2]) + oneq_ref[...]
        _store_v(outs[2], _with_aug(fq_ref[...], qaug), qt)


def _aug_tables():
    w = H_FOX * LANE
    ek = [[[0.0] * w for _ in range(LANE)] for _ in range(3)]
    eq = [[[0.0] * w for _ in range(LANE)] for _ in range(3)]
    onek = [0.0] * w
    oneq = [0.0] * w
    for hh in range(H_FOX):
        base = hh * LANE + FOX_DH
        for s in range(3):
            ek[s][hh][base + s] = -1.0
            eq[s][hh][base + 3 + s] = 1.0
            onek[base + 3 + s] = 1.0
            oneq[base + s] = 1.0
    return (jnp.array(ek, BF16), jnp.array(eq, BF16),
            jnp.array([onek], F32), jnp.array([oneq], F32))


def _cum_aug(lf_slab, init, fk, fq, t, qt=False):
    b, l, _ = lf_slab.shape
    nt = l // t
    tri = (lax.broadcasted_iota(jnp.int32, (t, t), 1)
           <= lax.broadcasted_iota(jnp.int32, (t, t), 0)).astype(BF16)
    consts = (tri,) + _aug_tables()
    w = H_FOX * LANE
    rows = lambda width: pl.BlockSpec((t, width), lambda i, j: (i * nt + j, 0))
    xs = [fk] + ([fq] if fq is not None else [])
    outs = [(pl.BlockSpec((1, t, LANE), lambda i, j: (i, j, 0)), jax.ShapeDtypeStruct((b, l, LANE), F32)),
            (rows(w), jax.ShapeDtypeStruct((b * l, w), BF16))]
    if fq is not None:
        outs.append(_v_out(b * l, w, t, qt, lambda i, j: i * nt + j))
    return pl.pallas_call(
        functools.partial(_cum_kernel, with_q=fq is not None, qt=qt),
        grid=(b, nt),
        in_specs=([pl.BlockSpec((1, t, LANE), lambda i, j: (i, j, 0)),
                   pl.BlockSpec((1, 1, LANE), lambda i, j: (i, 0, 0))]
                  + [_const_spec(c.shape) for c in consts] + [rows(x.shape[1]) for x in xs]),
        out_specs=[o[0] for o in outs],
        out_shape=[o[1] for o in outs],
        scratch_shapes=[pltpu.VMEM((1, LANE), F32)],
        compiler_params=_cparams("arbitrary", "arbitrary"),
        name="cum_aug",
    )(lf_slab, init, *consts, *xs)


def _softmax_step(qs, ks, v, mask, m_ref, l_ref, acc_ref):
    tk = v.shape[0]
    for j in range(2):
        s = lax.dot_general(qs[j], ks[j], (((1,), (1,)), ((), ())), preferred_element_type=F32)
        if mask is not None:
            s = jnp.where(mask, s, NEG)
        m_prev = m_ref[j]
        m_next = jnp.maximum(m_prev, jnp.max(s, axis=1, keepdims=True))
        p = jnp.exp2(s - jnp.concatenate([m_next] * (tk // LANE), axis=1)) if tk >= LANE \
            else jnp.exp2(s - m_next[:, :tk])
        alpha = jnp.exp2(m_prev - m_next)
        l_ref[j] = alpha * l_ref[j] + jnp.sum(p, axis=1, keepdims=True)
        m_ref[j] = m_next
        acc_ref[j] = acc_ref[j] * alpha + _dot(p.astype(BF16), v)


def _init_stats(m_ref, l_ref, acc_ref):
    m_ref[...] = jnp.full(m_ref.shape, NEG, F32)
    l_ref[...] = jnp.zeros(l_ref.shape, F32)
    acc_ref[...] = jnp.zeros(acc_ref.shape, F32)


def _half_masks(shape):
    lane = lax.broadcasted_iota(jnp.int32, shape, 1)
    return (lane % LANE) < (LANE // 2)


def _queries(q):
    return [q[:, :LANE], q[:, LANE:]]


def _keys(mode, k):
    return [k, k] if mode == "diff" else [k[:, :LANE], k[:, LANE:]]


def _visible(mode, q_pos, k_pos):
    if mode == "fox":
        return k_pos <= q_pos
    return (k_pos // CHUNK) <= (q_pos // CHUNK)


def _finish(mode, l_ref, acc_ref, extra, lam_init):
    o0 = acc_ref[0] / l_ref[0]
    o1 = acc_ref[1] / l_ref[1]
    if mode != "diff":
        return jnp.where(_half_masks(o0.shape), o0, o1)
    lam_ref, gsub_ref = extra
    lp = lam_ref[...]
    lam = (jnp.exp(jnp.sum(lp[0:1] * lp[1:2], axis=1, keepdims=True))
           - jnp.exp(jnp.sum(lp[2:3] * lp[3:4], axis=1, keepdims=True)) + lam_init)
    o = o0 - lam * o1
    return _rms(o, gsub_ref[...], 1e-5) * (1.0 - lam_init)


def _logits_t(qs, ks):
    return tuple(lax.dot_general(ks[j], qs[j], (((1,), (1,)), ((), ())), preferred_element_type=F32)
                 for j in range(2))


ONES_ROWS = 16


def _softmax_step_t(st, mx, vt1, mask, m_ref, acc_ref):
    if mask is not None:
        st = jnp.where(mask, st, NEG)
        mx = jnp.max(st, axis=0, keepdims=True)
    m_prev = m_ref[...]
    m_next = jnp.maximum(m_prev, mx)
    pt = jnp.exp2(st - m_next)
    alpha = jnp.exp2(m_prev - m_next)
    m_ref[...] = m_next
    acc_ref[...] = acc_ref[...] * alpha + _dot(vt1, pt.astype(BF16))


def _finish_t(mode, acc_ref, extra, lam_init):
    rows = acc_ref.shape[1] - ONES_ROWS
    o0 = acc_ref[0, :rows] / acc_ref[0, rows:rows + 1]
    o1 = acc_ref[1, :rows] / acc_ref[1, rows:rows + 1]
    if mode != "diff":
        return jnp.concatenate([o0, o1], axis=0).T
    lam_ref, gsub_ref = extra
    lp = lam_ref[...]
    lam = (jnp.exp(jnp.sum(lp[0:1] * lp[1:2], axis=1, keepdims=True))
           - jnp.exp(jnp.sum(lp[2:3] * lp[3:4], axis=1, keepdims=True)) + lam_init)
    o = (o0 - lam * o1).T
    return _rms(o, gsub_ref[...], 1e-5) * (1.0 - lam_init)


def _flash_kernel(qt_ref, k_ref, vt_ref, *rest, mode, nq, t, lam_init):
    n_extra = 2 if mode == "diff" else 0
    extra = rest[:n_extra]
    o_ref = rest[n_extra]
    m_ref, acc_ref, st_ref, mx_ref = rest[n_extra + 1:]
    half = LANE // 2
    rows_of = lambda blk: pl.ds(pl.multiple_of(blk * t, t), t)

    def produce(task, slot, j):
        qi, kb = task
        k = _keys(mode, k_ref[rows_of(kb), :])[j]
        st = _dot(k, qt_ref[j, qi])
        st_ref[slot, j] = st
        mx_ref[slot, j] = jnp.max(st, axis=0, keepdims=True)

    def fold(kb, slot, j, diagonal):
        vt = vt_ref[kb]
        if mode != "diff":
            vt = vt[j * half:(j + 1) * half]
        vt1 = jnp.concatenate([vt, jnp.ones((ONES_ROWS, t), BF16)], axis=0)
        msk = _visible(mode, lax.broadcasted_iota(jnp.int32, (t, t), 1),
                       lax.broadcasted_iota(jnp.int32, (t, t), 0)) if diagonal else None
        _softmax_step_t(st_ref[slot, j], mx_ref[slot, j], vt1, msk, m_ref.at[j], acc_ref.at[j])

    def both(cur, slot, nxt, diagonal):
        for j in range(2):
            produce(nxt, 1 - slot, j)
            fold(cur[1], slot, j, diagonal)

    def stage(cur, slot, nxt):
        qi, kb = cur

        @pl.when(kb == 0)
        def _():
            m_ref[...] = jnp.full(m_ref.shape, NEG, F32)
            acc_ref[...] = jnp.zeros(acc_ref.shape, F32)

        @pl.when(kb != qi)
        def _():
            both(cur, slot, nxt, False)

        @pl.when(kb == qi)
        def _():
            both(cur, slot, nxt, True)
            o_ref[rows_of(qi), :] = _finish_t(mode, acc_ref, extra, lam_init).astype(o_ref.dtype)

    def succ(task):
        qi, kb = task
        diag = kb == qi
        nqi = jnp.where(diag, qi + 1, qi)
        over = nqi >= nq
        return jnp.where(over, 0, nqi), jnp.where(diag | over, 0, kb + 1)

    first = (jnp.int32(0), jnp.int32(0))
    for j in range(2):
        produce(first, 0, j)

    def pair(_, task):
        nxt = succ(task)
        nxt2 = succ(nxt)
        stage(task, 0, nxt)
        stage(nxt, 1, nxt2)
        return nxt2

    n_tasks = nq * (nq + 1) // 2
    last = lax.fori_loop(0, n_tasks // 2, pair, first)
    if n_tasks % 2:
        stage(last, 0, first)


def _flash(mode, qt, k, vt, extra, b, s, t, lam_init=0.0):
    p = vt.shape[0]
    wk = k.shape[1] // p
    rows = LANE if mode == "diff" else LANE // 2
    seq = lambda w: pl.BlockSpec((s, w), lambda bi, pi: (bi, pi))
    return pl.pallas_call(
        functools.partial(_flash_kernel, mode=mode, nq=s // t, t=t, lam_init=lam_init),
        grid=(b, p),
        in_specs=([pl.BlockSpec((2, s // t, LANE, t), lambda bi, pi: (pi, bi, 0, 0)), seq(wk),
                   pl.BlockSpec((None, s // t, LANE, t), lambda bi, pi: (pi, bi, 0, 0))]
                  + [_const_spec(e.shape) for e in extra]),
        out_specs=seq(LANE),
        out_shape=jax.ShapeDtypeStruct((b * s, p * LANE), BF16),
        scratch_shapes=[pltpu.VMEM((2, 1, t), F32),
                        pltpu.VMEM((2, rows + ONES_ROWS, t), F32),
                        pltpu.VMEM((2, 2, t, t), F32), pltpu.VMEM((2, 2, 1, t), F32)],
        compiler_params=_cparams("parallel", "parallel"),
        name="flash_" + mode,
    )(qt, k, vt, *extra)


def _sample_kernel(q_ref, kp_ref, kn_ref, vp_ref, vn_ref, *rest, mode, past, tq, tkp, lam_init):
    n_extra = 2 if mode == "diff" else 0
    extra = rest[:n_extra]
    o_ref = rest[n_extra]
    m_ref, l_ref, acc_ref = rest[n_extra + 1:]
    _init_stats(m_ref, l_ref, acc_ref)
    new = lambda r: (r[0] if len(r.shape) == 3 else r[...])[:tq]
    qs = _queries(new(q_ref))
    for c in range(past // tkp):
        sl = slice(c * tkp, (c + 1) * tkp)
        ks = _keys(mode, kp_ref[0, sl, :].astype(BF16))
        _softmax_step(qs, ks, vp_ref[0, sl, :].astype(BF16), None, m_ref, l_ref, acc_ref)
    pad = lambda a: jnp.concatenate([a, jnp.zeros((LANE - tq, a.shape[1]), a.dtype)], axis=0)
    q_pos = past + lax.broadcasted_iota(jnp.int32, (tq, LANE), 0)
    k_idx = lax.broadcasted_iota(jnp.int32, (tq, LANE), 1)
    mask = _visible(mode, q_pos, past + k_idx) & (k_idx < tq)
    _softmax_step(qs, _keys(mode, pad(new(kn_ref))), pad(vn_ref[...]), mask, m_ref, l_ref, acc_ref)
    o_ref[...] = _finish(mode, l_ref, acc_ref, extra, lam_init).astype(o_ref.dtype)


def _sample_attn(mode, q, kp, kn, vp, vn, extra, bs, tq, past, lam_init=0.0):
    p = vn.shape[1] // LANE
    tkp = min(past, 1024)

    def new_spec(a):
        w = a.shape[-1] // p
        if a.ndim == 3:
            return pl.BlockSpec((1, a.shape[1], w), lambda bi, pi: (bi, 0, pi))
        return pl.BlockSpec((tq, w), lambda bi, pi: (bi, pi))

    def past_spec(a):
        return pl.BlockSpec((1, past, a.shape[-1] // p), lambda bi, pi: (bi, 0, pi))

    return pl.pallas_call(
        functools.partial(_sample_kernel, mode=mode, past=past, tq=tq, tkp=tkp, lam_init=lam_init),
        grid=(bs, p),
        in_specs=([new_spec(q), past_spec(kp), new_spec(kn), past_spec(vp), new_spec(vn)]
                  + [_const_spec(e.shape) for e in extra]),
        out_specs=pl.BlockSpec((tq, LANE), lambda bi, pi: (bi, pi)),
        out_shape=jax.ShapeDtypeStruct((bs * tq, p * LANE), BF16),
        scratch_shapes=[pltpu.VMEM((2, tq, LANE), F32)] * 3,
        compiler_params=_cparams("parallel", "parallel"),
        name="sample_" + mode,
    )(q, kp, kn, vp, vn, *extra)


def _kv_up_kernel(ckv_ref, kr_ref, wkv_ref, k_o, v_o):
    kin = jnp.concatenate([ckv_ref[...].astype(BF16), kr_ref[...].astype(BF16)], axis=1)
    kv = _dot(kin, wkv_ref[...])
    k_o[...] = kv[:, :H_MLA * LANE].astype(BF16)
    v_o[...] = kv[:, H_MLA * LANE:].astype(BF16)


def _kv_up(ckv, kr_slab, wkv_p, tm):
    n = ckv.shape[0]
    return pl.pallas_call(
        _kv_up_kernel,
        grid=(n // tm,),
        in_specs=[pl.BlockSpec((tm, KV_LORA), lambda i: (i, 0)),
                  pl.BlockSpec((tm, LANE), lambda i: (i, 0)),
                  _const_spec(wkv_p.shape)],
        out_specs=[pl.BlockSpec((tm, H_MLA * LANE), lambda i: (i, 0)),
                   pl.BlockSpec((tm, H_MLA * MLA_V), lambda i: (i, 0))],
        out_shape=[jax.ShapeDtypeStruct((n, H_MLA * LANE), BF16),
                   jax.ShapeDtypeStruct((n, H_MLA * MLA_V), BF16)],
        compiler_params=_cparams("parallel"),
        name="kv_up",
    )(ckv, kr_slab, wkv_p)


def _mlp_kernel(*refs, n_o, final, ff_chunk):
    o_refs = refs[:n_o]
    x_ref, wout_ref, gmlp_ref, wup_ref, wdown_ref = refs[n_o:n_o + 5]
    gfin_ref = refs[n_o + 5] if final else None
    out_ref = refs[-1]
    proj = None
    r0 = 0
    for o_ref in o_refs:
        w = o_ref.shape[1]
        t = _dot(o_ref[...], wout_ref[r0:r0 + w, :])
        proj = t if proj is None else proj + t
        r0 += w
    x1 = x_ref[...] + proj
    h = _rms(x1, gmlp_ref[...], 1e-6).astype(BF16)
    down = None
    for c in range(wup_ref.shape[1] // ff_chunk):
        u = _dot(h, wup_ref[:, c * ff_chunk:(c + 1) * ff_chunk])
        a = jnp.square(jnp.maximum(u, 0.0)).astype(BF16)
        t = _dot(a, wdown_ref[c * ff_chunk:(c + 1) * ff_chunk, :])
        down = t if down is None else down + t
    x2 = x1 + down
    out_ref[...] = _rms(x2, gfin_ref[...], 1e-6) if final else x2


def _outproj_mlp(os_, x, wout, gmlp, wup, wdown, gfin, tm):
    n, d = x.shape
    final = gfin is not None
    consts = [wout, gmlp, wup, wdown] + ([gfin] if final else [])
    return pl.pallas_call(
        functools.partial(_mlp_kernel, n_o=len(os_), final=final, ff_chunk=1024),
        grid=(n // tm,),
        in_specs=([pl.BlockSpec((tm, o.shape[1]), lambda i: (i, 0)) for o in os_]
                  + [pl.BlockSpec((tm, d), lambda i: (i, 0))]
                  + [_const_spec(c.shape) for c in consts]),
        out_specs=pl.BlockSpec((tm, d), lambda i: (i, 0)),
        out_shape=jax.ShapeDtypeStruct((n, d), F32),
        compiler_params=_cparams("parallel"),
        name="outproj_mlp",
    )(*os_, x, *consts)


def _proj_c_kernel(x_ref, tab_ref, gmix_ref, w_ref, k_o, v_o, qb_o, kb_o, vb_o, *, vt):
    h = _rms(x_ref[...], gmix_ref[...], 1e-6).astype(BF16)
    tab = tab_ref[...]
    cw = H_DIFF * 2 * DIFF_DH
    q = _rope_wide(_dot(h, w_ref[:, :cw]), tab, 0, DIFF_ROT // 2)
    first = lax.broadcasted_iota(jnp.int32, (q.shape[0], LANE), 1) < DIFF_DH
    slabs = []
    for hh in range(H_DIFF):
        qh = q[:, hh * LANE:(hh + 1) * LANE]
        slabs += [jnp.where(first, qh, 0.0), jnp.where(first, 0.0, qh)]
    _store_v(qb_o, jnp.concatenate(slabs, axis=1), vt)
    k = _rope_wide(_dot(h, w_ref[:, cw:2 * cw]), tab, 3 * LANE, DIFF_ROT // 2)
    kb_o[...] = k.astype(BF16)
    v = _dot(h, w_ref[:, 2 * cw:])
    _store_v(vb_o, v, vt)
    k_o[...] = pltpu.einshape("m(hd)->mhd", k, h=H_DIFF)
    v_o[...] = pltpu.einshape("m(hd)->mhd", v, h=H_DIFF)


def _proj_c(x, tab, tab_period, gmix, w, tm, vt):
    n, d = x.shape
    nper = tab_period // tm
    cw = H_DIFF * 2 * DIFF_DH
    plain = lambda t, w=cw: (pl.BlockSpec((tm, w), lambda i: (i, 0)), jax.ShapeDtypeStruct((n, w), t))
    heads = lambda: (pl.BlockSpec((tm, H_DIFF, LANE), lambda i: (i, 0, 0)),
                     jax.ShapeDtypeStruct((n, H_DIFF, LANE), F32))
    outs = [heads(), heads(), _v_out(n, 2 * cw, tm, vt), plain(BF16), _v_out(n, cw, tm, vt)]
    return pl.pallas_call(
        functools.partial(_proj_c_kernel, vt=vt),
        grid=(n // tm,),
        in_specs=[pl.BlockSpec((tm, d), lambda i: (i, 0)),
                  pl.BlockSpec((tm, 6 * LANE), lambda i: (i % nper, 0)),
                  _const_spec(gmix.shape), _const_spec(w.shape)],
        out_specs=[o[0] for o in outs],
        out_shape=[o[1] for o in outs],
        compiler_params=_cparams("parallel"),
        name="proj_c",
    )(x, tab, gmix, w)


def _angles(pos, half):
    inv = ROPE_THETA ** (-jnp.arange(half, dtype=F32) / half)
    ang = pos.astype(F32)[:, None] * inv[None, :]
    return jnp.cos(ang), jnp.sin(ang)


def _tables_a(pos):
    cos, sin = _angles(pos, MLA_ROPE // 2)
    t = pos.shape[0]
    z = lambda w: jnp.zeros((t, w), F32)
    sc = (MLA_NOPE + MLA_ROPE) ** -0.5 * LOG2E
    cq = sc * jnp.concatenate([jnp.ones((t, MLA_NOPE), F32), cos, cos, z(32)], axis=1)
    s1q = sc * jnp.concatenate([z(MLA_NOPE + 16), sin, z(32)], axis=1)
    s2q = sc * jnp.concatenate([z(MLA_NOPE), -sin, z(48)], axis=1)
    ck = jnp.concatenate([cos, cos, z(96)], axis=1)
    s1k = jnp.concatenate([z(16), sin, z(96)], axis=1)
    s2k = jnp.concatenate([-sin, z(112)], axis=1)
    return jnp.concatenate([cq, s1q, s2q, ck, s1k, s2k], axis=1)


def _tables_c(pos):
    cos, sin = _angles(pos, DIFF_ROT // 2)
    t = pos.shape[0]
    z = lambda w: jnp.zeros((t, w), F32)
    two = lambda a: jnp.concatenate([a, a], axis=1)
    c = two(jnp.concatenate([cos, cos, jnp.ones((t, 48), F32)], axis=1))
    s1 = two(jnp.concatenate([z(8), sin, z(48)], axis=1))
    s2 = two(jnp.concatenate([-sin, z(56)], axis=1))
    sc = DIFF_DH ** -0.5 * LOG2E
    return jnp.concatenate([sc * c, sc * s1, sc * s2, c, s1, s2], axis=1)


def _pad_cols(a, w):
    return jnp.pad(a, ((0, 0), (0, w - a.shape[1])))


def _prep_a(w_in, w_uq, w_ukv, b_f):
    o1, o2, o3 = Q_LORA, Q_LORA + KV_LORA, Q_LORA + KV_LORA + MLA_ROPE
    o4, o5, o6 = o3 + 512, o3 + 1024, o3 + 1536
    win_p = jnp.concatenate(
        [w_in[:, :o2], w_in[:, o3:o6], _pad_cols(w_in[:, o2:o3], LANE), _pad_cols(w_in[:, o6:], LANE)],
        axis=1).astype(BF16)
    dq = MLA_NOPE + MLA_ROPE
    wuq_p = jnp.pad(w_uq.reshape(Q_LORA, H_MLA, dq), ((0, 0), (0, 0), (0, LANE - dq)))
    wuq_p = wuq_p.reshape(Q_LORA, H_MLA * LANE).astype(BF16)
    wkv3 = w_ukv.reshape(KV_LORA, H_MLA, MLA_NOPE + MLA_V)
    wk = jnp.pad(wkv3[:, :, :MLA_NOPE], ((0, 0), (0, 0), (0, LANE - MLA_NOPE))).reshape(KV_LORA, H_MLA * LANE)
    wv = wkv3[:, :, MLA_NOPE:].reshape(KV_LORA, H_MLA * MLA_V)
    place = jnp.pad(jnp.eye(MLA_ROPE, dtype=F32), ((0, LANE - MLA_ROPE), (MLA_NOPE, LANE - MLA_NOPE - MLA_ROPE)))
    place = jnp.concatenate([jnp.tile(place, (1, H_MLA)), jnp.zeros((LANE, H_MLA * MLA_V), F32)], axis=1)
    wkv_p = jnp.concatenate([jnp.concatenate([wk, wv], axis=1), place], axis=0).astype(BF16)
    bf_slab = _pad_cols(b_f[None, :], LANE)
    return win_p, wuq_p, wkv_p, bf_slab


def _row_tile(n, want):
    t = min(n, want)
    while n % t:
        t //= 2
    return t


def kernel(x_prompt, x_sample, cache_mla_ckv, cache_mla_krope, cache_fox_k, cache_fox_v, cache_fox_logf,
           cache_diff_k, cache_diff_v, g_mix, a_w_in, a_g_q, a_w_uq, a_g_kv, a_w_ukv, a_b_f, a_w_out,
           c_w_in, c_lam, c_g_sub, c_w_out, g_mlp, w_up, w_down, g_final):
    b, s, d = x_prompt.shape
    bs, ts, _ = x_sample.shape
    past = cache_mla_ckv.shape[2]
    n, ns = b * s, bs * ts
    tm = _row_tile(s, 512)
    tq = tk = tm
    tms = _row_tile(ns, 512)
    lam_init = 0.8 - 0.6 * math.exp(-0.3 * 1)
    row = lambda a: a[None, :]

    xp = x_prompt.reshape(n, d)
    xs = x_sample.reshape(ns, d)
    pos_p = jnp.arange(s)
    pos_s = past + jnp.arange(ts)

    win_p, wuq_p, wkv_p, bf_slab = _prep_a(a_w_in[0], a_w_uq[0], a_w_ukv[0], a_b_f[0])
    wa = (row(g_mix[0]), win_p, row(a_g_q[0]), wuq_p, row(a_g_kv[0]), wkv_p, bf_slab)
    tab_ap = _tables_a(pos_p)
    tab_as = jnp.tile(_tables_a(pos_s), (tms // ts, 1))
    (ckv_p, krope_p, fk_p, fv_p, logf_p, lfs_p, qm_p, km_p, vm_p, fq_p, fkb_p, fvb_p) = _proj_a(
        xp, tab_ap, s, *wa, tm, True)
    (ckv_s, krope_s, fk_s, fv_s, logf_s, lfs_s, qm_s, km_s, vm_s, fq_s, fkb_s, fvb_s) = _proj_a(
        xs, tab_as, tms, *wa, tms, False)

    _, fks_p, fqs_p = _cum_aug(lfs_p.reshape(b, s, LANE), jnp.zeros((b, 1, LANE), F32), fkb_p, fq_p, tm,
                               qt=True)
    om_p = _flash("mla", qm_p, km_p, vm_p, [], b, s, tq)
    of_p = _flash("fox", fqs_p, fks_p, fvb_p, [], b, s, tq)

    lf_past = jnp.pad(cache_fox_logf[0], ((0, 0), (0, 0), (0, LANE - H_FOX)))
    cum_past, fks_past = _cum_aug(lf_past, jnp.zeros((bs, 1, LANE), F32),
                                  cache_fox_k[0].reshape(bs * past, -1), None, _row_tile(past, 512))
    rpad = 128
    pad_rows = lambda a: jnp.pad(a.reshape(bs, ts, -1), ((0, 0), (0, rpad - ts), (0, 0)))
    _, fks_new, fqs_new = _cum_aug(pad_rows(lfs_s), cum_past[:, past - 1:past, :],
                                   pad_rows(fkb_s).reshape(bs * rpad, -1),
                                   pad_rows(fq_s).reshape(bs * rpad, -1), rpad)
    kr_past = jnp.pad(cache_mla_krope[0].reshape(bs * past, MLA_ROPE), ((0, 0), (0, LANE - MLA_ROPE)))
    km_past, vm_past = _kv_up(cache_mla_ckv[0].reshape(bs * past, KV_LORA), kr_past, wkv_p,
                              _row_tile(bs * past, 1024))
    om_s = _sample_attn("mla", qm_s, km_past.reshape(bs, past, -1), km_s,
                        vm_past.reshape(bs, past, -1), vm_s, [], bs, ts, past)
    of_s = _sample_attn("fox", fqs_new.reshape(bs, rpad, -1), fks_past.reshape(bs, past, -1),
                        fks_new.reshape(bs, rpad, -1), cache_fox_v[0].reshape(bs, past, -1), fvb_s,
                        [], bs, ts, past)

    wout_a = a_w_out[0].astype(BF16)
    wup0, wdown0 = w_up[0].astype(BF16), w_down[0].astype(BF16)
    x1_p = _outproj_mlp([om_p, of_p], xp, wout_a, row(g_mlp[0]), wup0, wdown0, None, tm)
    x1_s = _outproj_mlp([om_s, of_s], xs, wout_a, row(g_mlp[0]), wup0, wdown0, None, tms)

    wc = c_w_in[0].astype(BF16)
    tab_cp = _tables_c(pos_p)
    tab_cs = jnp.tile(_tables_c(pos_s), (tms // ts, 1))
    dk_p, dv_p, dqb_p, dkb_p, dvb_p = _proj_c(x1_p, tab_cp, s, row(g_mix[1]), wc, tm, True)
    dk_s, dv_s, dqb_s, dkb_s, dvb_s = _proj_c(x1_s, tab_cs, tms, row(g_mix[1]), wc, tms, False)
    extra = [c_lam[0], row(c_g_sub[0])]
    oc_p = _flash("diff", dqb_p, dkb_p, dvb_p, extra, b, s, tq, lam_init)
    oc_s = _sample_attn("diff", dqb_s, cache_diff_k[0].reshape(bs, past, -1), dkb_s,
                        cache_diff_v[0].reshape(bs, past, -1), dvb_s, extra, bs, ts, past, lam_init)
    wout_c = c_w_out[0].astype(BF16)
    wup1, wdown1 = w_up[1].astype(BF16), w_down[1].astype(BF16)
    y_p = _outproj_mlp([oc_p], x1_p, wout_c, row(g_mlp[1]), wup1, wdown1, row(g_final), tm)
    y_s = _outproj_mlp([oc_s], x1_s, wout_c, row(g_mlp[1]), wup1, wdown1, row(g_final), tms)

    def pack(bb, tt, ckv, krope, fk, fv, logf, dk, dv):
        return (ckv.reshape(1, bb, tt, KV_LORA), krope.reshape(1, bb, tt, MLA_ROPE),
                fk.reshape(1, bb, tt, H_FOX, FOX_DH), fv.reshape(1, bb, tt, H_FOX, FOX_DH),
                logf.reshape(1, bb, tt, H_FOX),
                dk.reshape(1, bb, tt, H_DIFF, 2 * DIFF_DH), dv.reshape(1, bb, tt, H_DIFF, 2 * DIFF_DH))

    return ((y_p.reshape(b, s, d), y_s.reshape(bs, ts, d))
            + pack(b, s, ckv_p, krope_p, fk_p, fv_p, logf_p, dk_p, dv_p)
            + pack(bs, ts, ckv_s, krope_s, fk_s, fv_s, logf_s, dk_s, dv_s))
```

```python
import functools
import math

import jax
import jax.numpy as jnp
from jax import lax
from jax.experimental import pallas as pl
from jax.experimental.pallas import tpu as pltpu

F32 = jnp.float32
BF16 = jnp.bfloat16

CHUNK = 64
ROPE_THETA = 500000.0
H_MLA, MLA_NOPE, MLA_ROPE, MLA_V = 8, 64, 32, 64
Q_LORA, KV_LORA = 512, 256
H_FOX, FOX_DH = 8, 64
H_DIFF, DIFF_DH = 8, 64
DIFF_ROT = DIFF_DH // 4
LANE = 128
LOG2E = math.log2(math.e)
NEG = -1e30
VMEM_LIMIT = 56 * 1024 * 1024


def _cparams(*sem):
    return pltpu.CompilerParams(dimension_semantics=sem, vmem_limit_bytes=VMEM_LIMIT)


def _const_spec(shape):
    nd = len(shape)
    return pl.BlockSpec(shape, lambda *_: (0,) * nd, pipeline_mode=pl.Buffered(1))


def _rms(x, g, eps):
    y = x * lax.rsqrt(jnp.mean(x * x, axis=-1, keepdims=True) + eps)
    return y * g


def _dot(a, b):
    return jnp.dot(a, b, preferred_element_type=F32)


def _rope_slab(x, c, s1, s2, half):
    return x * c + pltpu.roll(x, half, 1) * s1 + pltpu.roll(x, LANE - half, 1) * s2


def _rope_wide(x, tab, base, half):
    c = tab[:, base:base + LANE]
    s1 = tab[:, base + LANE:base + 2 * LANE]
    s2 = tab[:, base + 2 * LANE:base + 3 * LANE]
    n = x.shape[1] // LANE
    return jnp.concatenate(
        [_rope_slab(x[:, j * LANE:(j + 1) * LANE], c, s1, s2, half) for j in range(n)], axis=1)


def _split3(x):
    hi = x.astype(BF16)
    r1 = x - hi.astype(F32)
    mid = r1.astype(BF16)
    lo = (r1 - mid.astype(F32)).astype(BF16)
    return hi, mid, lo


_O_Q, _O_CKV, _O_FQ, _O_FK, _O_FV, _O_KR, _O_F, _A_COLS = 0, 512, 768, 1280, 1792, 2304, 2432, 2560


def _store_v(v_o, v, vt):
    if not vt:
        v_o[...] = v.astype(BF16)
        return
    for p in range(v.shape[1] // LANE):
        v_o[p, 0] = v[:, p * LANE:(p + 1) * LANE].T.astype(BF16)


def _v_out(n, width, tm, vt, row_block=lambda i: i):
    if not vt:
        return (pl.BlockSpec((tm, width), lambda *g: (row_block(*g), 0)),
                jax.ShapeDtypeStruct((n, width), BF16))
    p = width // LANE
    return (pl.BlockSpec((p, 1, LANE, tm), lambda *g: (0, row_block(*g), 0, 0)),
            jax.ShapeDtypeStruct((p, n // tm, LANE, tm), BF16))


def _proj_a_kernel(x_ref, tab_ref, gmix_ref, win_ref, gq_ref, wuq_ref, gkv_ref, wkv_ref, bf_ref,
                   ckv_o, krope_o, fk_o, fv_o, logf_o, lfs_o, qm_o, km_o, vm_o, fq_o, fkb_o, fvb_o,
                   *, fox_scale, vt):
    h = _rms(x_ref[...], gmix_ref[...], 1e-6).astype(BF16)
    tab = tab_ref[...]
    ql = _dot(h, win_ref[:, _O_Q:_O_Q + Q_LORA])
    qn = _rms(ql, gq_ref[...], 1e-6).astype(BF16)
    q = _dot(qn, wuq_ref[...])
    _store_v(qm_o, _rope_wide(q, tab, 0, MLA_ROPE // 2), vt)
    cn = _rms(_dot(h, win_ref[:, _O_CKV:_O_CKV + KV_LORA]), gkv_ref[...], 1e-6)
    ckv_o[...] = cn
    kr = _rope_wide(_dot(h, win_ref[:, _O_KR:_O_KR + LANE]), tab, 3 * LANE, MLA_ROPE // 2)
    krope_o[...] = kr[:, :MLA_ROPE]
    kin = jnp.concatenate([cn.astype(BF16), kr.astype(BF16)], axis=1)
    kv = _dot(kin, wkv_ref[...])
    km_o[...] = kv[:, :H_MLA * LANE].astype(BF16)
    _store_v(vm_o, kv[:, H_MLA * LANE:], vt)
    fq_o[...] = (_dot(h, win_ref[:, _O_FQ:_O_FQ + 512]) * fox_scale).astype(BF16)
    fk = _dot(h, win_ref[:, _O_FK:_O_FK + 512])
    fkb_o[...] = fk.astype(BF16)
    fv = _dot(h, win_ref[:, _O_FV:_O_FV + 512])
    _store_v(fvb_o, fv, vt)
    fk_o[...] = pltpu.einshape("m(hd)->mhd", fk, h=H_FOX)
    fv_o[...] = pltpu.einshape("m(hd)->mhd", fv, h=H_FOX)
    z = _dot(h, win_ref[:, _O_F:_O_F + LANE]) + bf_ref[...]
    ls = jnp.minimum(z, 0.0) - jnp.log1p(jnp.exp(-jnp.abs(z)))
    lane = lax.broadcasted_iota(jnp.int32, ls.shape, 1)
    ls = jnp.where(lane < H_FOX, ls, 0.0)
    lfs_o[...] = ls
    logf_o[...] = ls[:, :H_FOX]


def _proj_a(x, tab, tab_period, gmix, win_p, gq, wuq_p, gkv, wkv_p, bf_slab, tm, vt):
    n, d = x.shape
    nt = n // tm
    nper = tab_period // tm
    row = lambda w: pl.BlockSpec((tm, w), lambda i: (i, 0))
    plain = lambda w, t: (row(w), jax.ShapeDtypeStruct((n, w), t))
    heads = lambda: (pl.BlockSpec((tm, H_FOX, FOX_DH), lambda i: (i, 0, 0)),
                     jax.ShapeDtypeStruct((n, H_FOX, FOX_DH), F32))
    outs = [plain(KV_LORA, F32), plain(MLA_ROPE, F32), heads(), heads(),
            plain(H_FOX, F32), plain(LANE, F32),
            _v_out(n, H_MLA * LANE, tm, vt), plain(H_MLA * LANE, BF16), _v_out(n, H_MLA * MLA_V, tm, vt),
            plain(512, BF16), plain(512, BF16), _v_out(n, 512, tm, vt)]
    return pl.pallas_call(
        functools.partial(_proj_a_kernel, fox_scale=FOX_DH ** -0.5 * LOG2E, vt=vt),
        grid=(nt,),
        in_specs=[row(d),
                  pl.BlockSpec((tm, 6 * LANE), lambda i: (i % nper, 0)),
                  _const_spec(gmix.shape), _const_spec(win_p.shape), _const_spec(gq.shape),
                  _const_spec(wuq_p.shape), _const_spec(gkv.shape), _const_spec(wkv_p.shape),
                  _const_spec(bf_slab.shape)],
        out_specs=[o[0] for o in outs],
        out_shape=[o[1] for o in outs],
        compiler_params=_cparams("parallel"),
        name="proj_a",
    )(x, tab, gmix, win_p, gq, wuq_p, gkv, wkv_p, bf_slab)


def _with_aug(x, aug):
    x = x.astype(F32)
    first = lax.broadcasted_iota(jnp.int32, (x.shape[0], LANE), 1) < LANE // 2
    out = []
    for p in range(H_FOX // 2):
        pair = x[:, p * LANE:(p + 1) * LANE]
        out.append(jnp.where(first, pair, aug[:, 2 * p * LANE:(2 * p + 1) * LANE]))
        out.append(jnp.where(first, pltpu.roll(pair, LANE // 2, 1),
                             aug[:, (2 * p + 1) * LANE:(2 * p + 2) * LANE]))
    return jnp.concatenate(out, axis=1)


def _cum_kernel(*refs, with_q, qt):
    lf_ref, init_ref, tri_ref, ek_ref, eq_ref, onek_ref, oneq_ref, fk_ref = refs[:8]
    fq_ref = refs[8] if with_q else None
    outs = refs[9 if with_q else 8:-1]
    carry = refs[-1]

    @pl.when(pl.program_id(1) == 0)
    def _():
        carry[...] = init_ref[0]

    pieces = lambda x: jnp.concatenate(_split3(x), axis=1)
    c3 = _dot(tri_ref[...], pieces(lf_ref[0]))
    cum = carry[...] + (c3[:, :LANE] + c3[:, LANE:2 * LANE] + c3[:, 2 * LANE:])
    t = cum.shape[0]
    carry[...] = cum[t - 1:t, :]
    outs[0][0] = cum
    p3 = pieces(cum * LOG2E)
    kaug = _dot(p3, ek_ref[...]) + onek_ref[...]
    outs[1][...] = _with_aug(fk_ref[...], kaug).astype(BF16)
    if with_q:
        qaug = _dot(p3, eq_ref[...]) + oneq_ref[...]
        _store_v(outs[2], _with_aug(fq_ref[...], qaug), qt)


def _aug_tables():
    w = H_FOX * LANE
    ek = [[[0.0] * w for _ in range(LANE)] for _ in range(3)]
    eq = [[[0.0] * w for _ in range(LANE)] for _ in range(3)]
    onek = [0.0] * w
    oneq = [0.0] * w
    for hh in range(H_FOX):
        base = hh * LANE + FOX_DH
        for s in range(3):
            ek[s][hh][base + s] = -1.0
            eq[s][hh][base + 3 + s] = 1.0
            onek[base + 3 + s] = 1.0
            oneq[base + s] = 1.0
    return (jnp.array(ek, BF16).reshape(3 * LANE, w), jnp.array(eq, BF16).reshape(3 * LANE, w),
            jnp.array([onek], F32), jnp.array([oneq], F32))


def _cum_aug(lf_slab, init, fk, fq, t, qt=False):
    b, l, _ = lf_slab.shape
    nt = l // t
    tri = (lax.broadcasted_iota(jnp.int32, (t, t), 1)
           <= lax.broadcasted_iota(jnp.int32, (t, t), 0)).astype(BF16)
    consts = (tri,) + _aug_tables()
    w = H_FOX * LANE
    rows = lambda width: pl.BlockSpec((t, width), lambda i, j: (i * nt + j, 0))
    xs = [fk] + ([fq] if fq is not None else [])
    outs = [(pl.BlockSpec((1, t, LANE), lambda i, j: (i, j, 0)), jax.ShapeDtypeStruct((b, l, LANE), F32)),
            (rows(w), jax.ShapeDtypeStruct((b * l, w), BF16))]
    if fq is not None:
        outs.append(_v_out(b * l, w, t, qt, lambda i, j: i * nt + j))
    return pl.pallas_call(
        functools.partial(_cum_kernel, with_q=fq is not None, qt=qt),
        grid=(b, nt),
        in_specs=([pl.BlockSpec((1, t, LANE), lambda i, j: (i, j, 0)),
                   pl.BlockSpec((1, 1, LANE), lambda i, j: (i, 0, 0))]
                  + [_const_spec(c.shape) for c in consts] + [rows(x.shape[1]) for x in xs]),
        out_specs=[o[0] for o in outs],
        out_shape=[o[1] for o in outs],
        scratch_shapes=[pltpu.VMEM((1, LANE), F32)],
        compiler_params=_cparams("arbitrary", "arbitrary"),
        name="cum_aug",
    )(lf_slab, init, *consts, *xs)


def _softmax_step(qs, ks, v, mask, m_ref, l_ref, acc_ref):
    tk = v.shape[0]
    for j in range(2):
        s = lax.dot_general(qs[j], ks[j], (((1,), (1,)), ((), ())), preferred_element_type=F32)
        if mask is not None:
            s = jnp.where(mask, s, NEG)
        m_prev = m_ref[j]
        m_next = jnp.maximum(m_prev, jnp.max(s, axis=1, keepdims=True))
        p = jnp.exp2(s - jnp.concatenate([m_next] * (tk // LANE), axis=1)) if tk >= LANE \
            else jnp.exp2(s - m_next[:, :tk])
        alpha = jnp.exp2(m_prev - m_next)
        l_ref[j] = alpha * l_ref[j] + jnp.sum(p, axis=1, keepdims=True)
        m_ref[j] = m_next
        acc_ref[j] = acc_ref[j] * alpha + _dot(p.astype(BF16), v)


def _init_stats(m_ref, l_ref, acc_ref):
    m_ref[...] = jnp.full(m_ref.shape, NEG, F32)
    l_ref[...] = jnp.zeros(l_ref.shape, F32)
    acc_ref[...] = jnp.zeros(acc_ref.shape, F32)


def _half_masks(shape):
    lane = lax.broadcasted_iota(jnp.int32, shape, 1)
    return (lane % LANE) < (LANE // 2)


def _queries(q):
    return [q[:, :LANE], q[:, LANE:]]


def _keys(mode, k):
    return [k, k] if mode == "diff" else [k[:, :LANE], k[:, LANE:]]


def _visible(mode, q_pos, k_pos):
    if mode == "fox":
        return k_pos <= q_pos
    return (k_pos // CHUNK) <= (q_pos // CHUNK)


def _finish(mode, l_ref, acc_ref, extra, lam_init):
    o0 = acc_ref[0] / l_ref[0]
    o1 = acc_ref[1] / l_ref[1]
    if mode != "diff":
        return jnp.where(_half_masks(o0.shape), o0, o1)
    lam_ref, gsub_ref = extra
    lp = lam_ref[...]
    lam = (jnp.exp(jnp.sum(lp[0:1] * lp[1:2], axis=1, keepdims=True))
           - jnp.exp(jnp.sum(lp[2:3] * lp[3:4], axis=1, keepdims=True)) + lam_init)
    o = o0 - lam * o1
    return _rms(o, gsub_ref[...], 1e-5) * (1.0 - lam_init)


ONES_ROWS = 16


def _softmax_step_t(st, mx, vt1, mask, m_ref, acc_ref):
    if mask is not None:
        st = jnp.where(mask, st, NEG)
        mx = jnp.max(st, axis=0, keepdims=True)
    m_prev = m_ref[...]
    m_next = jnp.maximum(m_prev, mx)
    pt = jnp.exp2(st - m_next)
    alpha = jnp.exp2(m_prev - m_next)
    m_ref[...] = m_next
    acc_ref[...] = acc_ref[...] * alpha + _dot(vt1, pt.astype(BF16))


def _finish_t(mode, acc_ref, extra, lam_init):
    rows = acc_ref.shape[1] - ONES_ROWS
    o0 = acc_ref[0, :rows] / acc_ref[0, rows:rows + 1]
    o1 = acc_ref[1, :rows] / acc_ref[1, rows:rows + 1]
    if mode != "diff":
        return jnp.concatenate([o0, o1], axis=0).T
    lam_ref, gsub_ref = extra
    lp = lam_ref[...]
    lam = (jnp.exp(jnp.sum(lp[0:1] * lp[1:2], axis=1, keepdims=True))
           - jnp.exp(jnp.sum(lp[2:3] * lp[3:4], axis=1, keepdims=True)) + lam_init)
    o = (o0 - lam * o1).T
    return _rms(o, gsub_ref[...], 1e-5) * (1.0 - lam_init)


def _flash_kernel(qt_ref, k_ref, vt_ref, *rest, mode, nq, t, lam_init):
    n_extra = 2 if mode == "diff" else 0
    extra = rest[:n_extra]
    o_ref = rest[n_extra]
    m_ref, acc_ref, st_ref, mx_ref = rest[n_extra + 1:]
    half = LANE // 2
    rows_of = lambda blk: pl.ds(pl.multiple_of(blk * t, t), t)

    def produce(task, slot, j):
        qi, kb = task
        k = _keys(mode, k_ref[rows_of(kb), :])[j]
        st = _dot(k, qt_ref[j, qi])
        st_ref[slot, j] = st
        mx_ref[slot, j] = jnp.max(st, axis=0, keepdims=True)

    def fold(kb, slot, j, diagonal):
        vt = vt_ref[kb]
        if mode != "diff":
            vt = vt[j * half:(j + 1) * half]
        vt1 = jnp.concatenate([vt, jnp.ones((ONES_ROWS, t), BF16)], axis=0)
        msk = _visible(mode, lax.broadcasted_iota(jnp.int32, (t, t), 1),
                       lax.broadcasted_iota(jnp.int32, (t, t), 0)) if diagonal else None
        _softmax_step_t(st_ref[slot, j], mx_ref[slot, j], vt1, msk, m_ref.at[j], acc_ref.at[j])

    def both(cur, slot, nxt, diagonal):
        for j in range(2):
            produce(nxt, 1 - slot, j)
            fold(cur[1], slot, j, diagonal)

    def stage(cur, slot, nxt):
        qi, kb = cur

        @pl.when(kb == 0)
        def _():
            m_ref[...] = jnp.full(m_ref.shape, NEG, F32)
            acc_ref[...] = jnp.zeros(acc_ref.shape, F32)

        @pl.when(kb != qi)
        def _():
            both(cur, slot, nxt, False)

        @pl.when(kb == qi)
        def _():
            both(cur, slot, nxt, True)
            o_ref[rows_of(qi), :] = _finish_t(mode, acc_ref, extra, lam_init).astype(o_ref.dtype)

    def succ(task):
        qi, kb = task
        diag = kb == qi
        nqi = jnp.where(diag, qi + 1, qi)
        over = nqi >= nq
        return jnp.where(over, 0, nqi), jnp.where(diag | over, 0, kb + 1)

    first = (jnp.int32(0), jnp.int32(0))
    for j in range(2):
        produce(first, 0, j)

    def pair(_, task):
        nxt = succ(task)
        nxt2 = succ(nxt)
        stage(task, 0, nxt)
        stage(nxt, 1, nxt2)
        return nxt2

    n_tasks = nq * (nq + 1) // 2
    last = lax.fori_loop(0, n_tasks // 2, pair, first)
    if n_tasks % 2:
        stage(last, 0, first)


def _flash(mode, qt, k, vt, extra, b, s, t, lam_init=0.0):
    p = vt.shape[0]
    wk = k.shape[1] // p
    rows = LANE if mode == "diff" else LANE // 2
    seq = lambda w: pl.BlockSpec((s, w), lambda bi, pi: (bi, pi))
    return pl.pallas_call(
        functools.partial(_flash_kernel, mode=mode, nq=s // t, t=t, lam_init=lam_init),
        grid=(b, p),
        in_specs=([pl.BlockSpec((2, s // t, LANE, t), lambda bi, pi: (pi, bi, 0, 0)), seq(wk),
                   pl.BlockSpec((None, s // t, LANE, t), lambda bi, pi: (pi, bi, 0, 0))]
                  + [_const_spec(e.shape) for e in extra]),
        out_specs=seq(LANE),
        out_shape=jax.ShapeDtypeStruct((b * s, p * LANE), BF16),
        scratch_shapes=[pltpu.VMEM((2, 1, t), F32),
                        pltpu.VMEM((2, rows + ONES_ROWS, t), F32),
                        pltpu.VMEM((2, 2, t, t), F32), pltpu.VMEM((2, 2, 1, t), F32)],
        compiler_params=_cparams("parallel", "parallel"),
        name="flash_" + mode,
    )(qt, k, vt, *extra)


def _sample_kernel(q_ref, kp_ref, kn_ref, vp_ref, vn_ref, *rest, mode, past, tq, tkp, lam_init):
    n_extra = 2 if mode == "diff" else 0
    extra = rest[:n_extra]
    o_ref = rest[n_extra]
    m_ref, l_ref, acc_ref = rest[n_extra + 1:]
    _init_stats(m_ref, l_ref, acc_ref)
    new = lambda r: (r[0] if len(r.shape) == 3 else r[...])[:tq]
    qs = _queries(new(q_ref))
    for c in range(past // tkp):
        sl = slice(c * tkp, (c + 1) * tkp)
        ks = _keys(mode, kp_ref[0, sl, :].astype(BF16))
        _softmax_step(qs, ks, vp_ref[0, sl, :].astype(BF16), None, m_ref, l_ref, acc_ref)
    pad = lambda a: jnp.concatenate([a, jnp.zeros((LANE - tq, a.shape[1]), a.dtype)], axis=0)
    q_pos = past + lax.broadcasted_iota(jnp.int32, (tq, LANE), 0)
    k_idx = lax.broadcasted_iota(jnp.int32, (tq, LANE), 1)
    mask = _visible(mode, q_pos, past + k_idx) & (k_idx < tq)
    _softmax_step(qs, _keys(mode, pad(new(kn_ref))), pad(vn_ref[...]), mask, m_ref, l_ref, acc_ref)
    o_ref[...] = _finish(mode, l_ref, acc_ref, extra, lam_init).astype(o_ref.dtype)


def _sample_attn(mode, q, kp, kn, vp, vn, extra, bs, tq, past, lam_init=0.0):
    p = vn.shape[1] // LANE
    tkp = min(past, 1024)

    def new_spec(a):
        w = a.shape[-1] // p
        if a.ndim == 3:
            return pl.BlockSpec((1, a.shape[1], w), lambda bi, pi: (bi, 0, pi))
        return pl.BlockSpec((tq, w), lambda bi, pi: (bi, pi))

    def past_spec(a):
        return pl.BlockSpec((1, past, a.shape[-1] // p), lambda bi, pi: (bi, 0, pi))

    return pl.pallas_call(
        functools.partial(_sample_kernel, mode=mode, past=past, tq=tq, tkp=tkp, lam_init=lam_init),
        grid=(bs, p),
        in_specs=([new_spec(q), past_spec(kp), new_spec(kn), past_spec(vp), new_spec(vn)]
                  + [_const_spec(e.shape) for e in extra]),
        out_specs=pl.BlockSpec((tq, LANE), lambda bi, pi: (bi, pi)),
        out_shape=jax.ShapeDtypeStruct((bs * tq, p * LANE), BF16),
        scratch_shapes=[pltpu.VMEM((2, tq, LANE), F32)] * 3,
        compiler_params=_cparams("parallel", "parallel"),
        name="sample_" + mode,
    )(q, kp, kn, vp, vn, *extra)


def _sample_heads_kernel(q_ref, kp_ref, kn_ref, vp_ref, vn_ref, lam_ref, gsub_ref, o_ref,
                         m_ref, l_ref, acc_ref, *, past, tq, tkp, lam_init):
    c = pl.program_id(1)

    @pl.when(c == 0)
    def _():
        _init_stats(m_ref, l_ref, acc_ref)

    def stats(hh):
        return [r.at[2 * hh:2 * hh + 2] for r in (m_ref, l_ref, acc_ref)]

    def queries(hh):
        return _queries(q_ref[:, 2 * hh * LANE:(2 * hh + 2) * LANE])

    for hh in range(H_DIFF):
        head_rows = pl.ds(hh, tkp, stride=H_DIFF)
        k = kp_ref[0, head_rows, :].astype(BF16)
        v = vp_ref[0, head_rows, :].astype(BF16)
        _softmax_step(queries(hh), [k, k], v, None, *stats(hh))

    @pl.when(c == pl.num_programs(1) - 1)
    def _():
        pad = lambda a: jnp.concatenate([a, jnp.zeros((LANE - tq, a.shape[1]), a.dtype)], axis=0)
        q_pos = past + lax.broadcasted_iota(jnp.int32, (tq, LANE), 0)
        k_idx = lax.broadcasted_iota(jnp.int32, (tq, LANE), 1)
        mask = _visible("diff", q_pos, past + k_idx) & (k_idx < tq)
        for hh in range(H_DIFF):
            sl = slice(hh * LANE, (hh + 1) * LANE)
            k = pad(kn_ref[:, sl])
            _softmax_step(queries(hh), [k, k], pad(vn_ref[:, sl]), mask, *stats(hh))
            m_h, l_h, acc_h = stats(hh)
            o_ref[:, sl] = _finish("diff", l_h, acc_h, (lam_ref, gsub_ref), lam_init).astype(o_ref.dtype)


def _sample_attn_heads(q, kp, kn, vp, vn, extra, bs, tq, past, lam_init):
    tkp = min(past, 1024)
    new = lambda a: pl.BlockSpec((tq, a.shape[1]), lambda bi, ci: (bi, 0))
    chunk = pl.BlockSpec((1, tkp * H_DIFF, LANE), lambda bi, ci: (bi, ci, 0))
    return pl.pallas_call(
        functools.partial(_sample_heads_kernel, past=past, tq=tq, tkp=tkp, lam_init=lam_init),
        grid=(bs, past // tkp),
        in_specs=[new(q), chunk, new(kn), chunk, new(vn)] + [_const_spec(e.shape) for e in extra],
        out_specs=pl.BlockSpec((tq, H_DIFF * LANE), lambda bi, ci: (bi, 0)),
        out_shape=jax.ShapeDtypeStruct((bs * tq, H_DIFF * LANE), BF16),
        scratch_shapes=[pltpu.VMEM((2 * H_DIFF, tq, LANE), F32)] * 3,
        compiler_params=_cparams("parallel", "arbitrary"),
        name="sample_diff",
    )(q, kp, kn, vp, vn, *extra)


def _kv_up_kernel(ckv_ref, kr_ref, wkv_ref, k_o, v_o):
    kin = jnp.concatenate([ckv_ref[...].astype(BF16), kr_ref[...].astype(BF16)], axis=1)
    kv = _dot(kin, wkv_ref[...])
    k_o[...] = kv[:, :H_MLA * LANE].astype(BF16)
    v_o[...] = kv[:, H_MLA * LANE:].astype(BF16)


def _kv_up(ckv, kr_slab, wkv_p, tm):
    n = ckv.shape[0]
    return pl.pallas_call(
        _kv_up_kernel,
        grid=(n // tm,),
        in_specs=[pl.BlockSpec((tm, KV_LORA), lambda i: (i, 0)),
                  pl.BlockSpec((tm, LANE), lambda i: (i, 0)),
                  _const_spec(wkv_p.shape)],
        out_specs=[pl.BlockSpec((tm, H_MLA * LANE), lambda i: (i, 0)),
                   pl.BlockSpec((tm, H_MLA * MLA_V), lambda i: (i, 0))],
        out_shape=[jax.ShapeDtypeStruct((n, H_MLA * LANE), BF16),
                   jax.ShapeDtypeStruct((n, H_MLA * MLA_V), BF16)],
        compiler_params=_cparams("parallel"),
        name="kv_up",
    )(ckv, kr_slab, wkv_p)


def _mlp_kernel(*refs, n_o, final, ff_chunk):
    o_refs = refs[:n_o]
    x_ref, wout_ref, gmlp_ref, wup_ref, wdown_ref = refs[n_o:n_o + 5]
    gfin_ref = refs[n_o + 5] if final else None
    out_ref = refs[-1]
    proj = None
    r0 = 0
    for o_ref in o_refs:
        w = o_ref.shape[1]
        t = _dot(o_ref[...], wout_ref[r0:r0 + w, :])
        proj = t if proj is None else proj + t
        r0 += w
    x1 = x_ref[...] + proj
    h = _rms(x1, gmlp_ref[...], 1e-6).astype(BF16)
    down = None
    for c in range(wup_ref.shape[1] // ff_chunk):
        u = _dot(h, wup_ref[:, c * ff_chunk:(c + 1) * ff_chunk])
        a = jnp.square(jnp.maximum(u, 0.0)).astype(BF16)
        t = _dot(a, wdown_ref[c * ff_chunk:(c + 1) * ff_chunk, :])
        down = t if down is None else down + t
    x2 = x1 + down
    out_ref[...] = _rms(x2, gfin_ref[...], 1e-6) if final else x2


def _outproj_mlp(os_, x, wout, gmlp, wup, wdown, gfin, tm):
    n, d = x.shape
    final = gfin is not None
    consts = [wout, gmlp, wup, wdown] + ([gfin] if final else [])
    return pl.pallas_call(
        functools.partial(_mlp_kernel, n_o=len(os_), final=final, ff_chunk=1024),
        grid=(n // tm,),
        in_specs=([pl.BlockSpec((tm, o.shape[1]), lambda i: (i, 0)) for o in os_]
                  + [pl.BlockSpec((tm, d), lambda i: (i, 0))]
                  + [_const_spec(c.shape) for c in consts]),
        out_specs=pl.BlockSpec((tm, d), lambda i: (i, 0)),
        out_shape=jax.ShapeDtypeStruct((n, d), F32),
        compiler_params=_cparams("parallel"),
        name="outproj_mlp",
    )(*os_, x, *consts)


def _proj_c_kernel(x_ref, tab_ref, gmix_ref, w_ref, k_o, v_o, qb_o, kb_o, vb_o, *, vt):
    h = _rms(x_ref[...], gmix_ref[...], 1e-6).astype(BF16)
    tab = tab_ref[...]
    cw = H_DIFF * 2 * DIFF_DH
    q = _rope_wide(_dot(h, w_ref[:, :cw]), tab, 0, DIFF_ROT // 2)
    first = lax.broadcasted_iota(jnp.int32, (q.shape[0], LANE), 1) < DIFF_DH
    slabs = []
    for hh in range(H_DIFF):
        qh = q[:, hh * LANE:(hh + 1) * LANE]
        slabs += [jnp.where(first, qh, 0.0), jnp.where(first, 0.0, qh)]
    _store_v(qb_o, jnp.concatenate(slabs, axis=1), vt)
    k = _rope_wide(_dot(h, w_ref[:, cw:2 * cw]), tab, 3 * LANE, DIFF_ROT // 2)
    kb_o[...] = k.astype(BF16)
    v = _dot(h, w_ref[:, 2 * cw:])
    _store_v(vb_o, v, vt)
    k_o[...] = pltpu.einshape("m(hd)->mhd", k, h=H_DIFF)
    v_o[...] = pltpu.einshape("m(hd)->mhd", v, h=H_DIFF)


def _proj_c(x, tab, tab_period, gmix, w, tm, vt):
    n, d = x.shape
    nper = tab_period // tm
    cw = H_DIFF * 2 * DIFF_DH
    plain = lambda t, w=cw: (pl.BlockSpec((tm, w), lambda i: (i, 0)), jax.ShapeDtypeStruct((n, w), t))
    heads = lambda: (pl.BlockSpec((tm, H_DIFF, LANE), lambda i: (i, 0, 0)),
                     jax.ShapeDtypeStruct((n, H_DIFF, LANE), F32))
    outs = [heads(), heads(), _v_out(n, 2 * cw, tm, vt), plain(BF16), _v_out(n, cw, tm, vt)]
    return pl.pallas_call(
        functools.partial(_proj_c_kernel, vt=vt),
        grid=(n // tm,),
        in_specs=[pl.BlockSpec((tm, d), lambda i: (i, 0)),
                  pl.BlockSpec((tm, 6 * LANE), lambda i: (i % nper, 0)),
                  _const_spec(gmix.shape), _const_spec(w.shape)],
        out_specs=[o[0] for o in outs],
        out_shape=[o[1] for o in outs],
        compiler_params=_cparams("parallel"),
        name="proj_c",
    )(x, tab, gmix, w)


def _angles(pos, half):
    inv = ROPE_THETA ** (-jnp.arange(half, dtype=F32) / half)
    ang = pos.astype(F32)[:, None] * inv[None, :]
    return jnp.cos(ang), jnp.sin(ang)


def _tables_a(pos):
    cos, sin = _angles(pos, MLA_ROPE // 2)
    t = pos.shape[0]
    z = lambda w: jnp.zeros((t, w), F32)
    sc = (MLA_NOPE + MLA_ROPE) ** -0.5 * LOG2E
    cq = sc * jnp.concatenate([jnp.ones((t, MLA_NOPE), F32), cos, cos, z(32)], axis=1)
    s1q = sc * jnp.concatenate([z(MLA_NOPE + 16), sin, z(32)], axis=1)
    s2q = sc * jnp.concatenate([z(MLA_NOPE), -sin, z(48)], axis=1)
    ck = jnp.concatenate([cos, cos, z(96)], axis=1)
    s1k = jnp.concatenate([z(16), sin, z(96)], axis=1)
    s2k = jnp.concatenate([-sin, z(112)], axis=1)
    return jnp.concatenate([cq, s1q, s2q, ck, s1k, s2k], axis=1)


def _tables_c(pos):
    cos, sin = _angles(pos, DIFF_ROT // 2)
    t = pos.shape[0]
    z = lambda w: jnp.zeros((t, w), F32)
    two = lambda a: jnp.concatenate([a, a], axis=1)
    c = two(jnp.concatenate([cos, cos, jnp.ones((t, 48), F32)], axis=1))
    s1 = two(jnp.concatenate([z(8), sin, z(48)], axis=1))
    s2 = two(jnp.concatenate([-sin, z(56)], axis=1))
    sc = DIFF_DH ** -0.5 * LOG2E
    return jnp.concatenate([sc * c, sc * s1, sc * s2, c, s1, s2], axis=1)


def _pad_cols(a, w):
    return jnp.pad(a, ((0, 0), (0, w - a.shape[1])))


def _prep_a(w_in, w_uq, w_ukv, b_f):
    o1, o2, o3 = Q_LORA, Q_LORA + KV_LORA, Q_LORA + KV_LORA + MLA_ROPE
    o4, o5, o6 = o3 + 512, o3 + 1024, o3 + 1536
    win_p = jnp.concatenate(
        [w_in[:, :o2], w_in[:, o3:o6], _pad_cols(w_in[:, o2:o3], LANE), _pad_cols(w_in[:, o6:], LANE)],
        axis=1).astype(BF16)
    dq = MLA_NOPE + MLA_ROPE
    wuq_p = jnp.pad(w_uq.reshape(Q_LORA, H_MLA, dq), ((0, 0), (0, 0), (0, LANE - dq)))
    wuq_p = wuq_p.reshape(Q_LORA, H_MLA * LANE).astype(BF16)
    wkv3 = w_ukv.reshape(KV_LORA, H_MLA, MLA_NOPE + MLA_V)
    wk = jnp.pad(wkv3[:, :, :MLA_NOPE], ((0, 0), (0, 0), (0, LANE - MLA_NOPE))).reshape(KV_LORA, H_MLA * LANE)
    wv = wkv3[:, :, MLA_NOPE:].reshape(KV_LORA, H_MLA * MLA_V)
    place = jnp.pad(jnp.eye(MLA_ROPE, dtype=F32), ((0, LANE - MLA_ROPE), (MLA_NOPE, LANE - MLA_NOPE - MLA_ROPE)))
    place = jnp.concatenate([jnp.tile(place, (1, H_MLA)), jnp.zeros((LANE, H_MLA * MLA_V), F32)], axis=1)
    wkv_p = jnp.concatenate([jnp.concatenate([wk, wv], axis=1), place], axis=0).astype(BF16)
    bf_slab = _pad_cols(b_f[None, :], LANE)
    return win_p, wuq_p, wkv_p, bf_slab


def _row_tile(n, want):
    t = min(n, want)
    while n % t:
        t //= 2
    return t


def kernel(x_prompt, x_sample, cache_mla_ckv, cache_mla_krope, cache_fox_k, cache_fox_v, cache_fox_logf,
           cache_diff_k, cache_diff_v, g_mix, a_w_in, a_g_q, a_w_uq, a_g_kv, a_w_ukv, a_b_f, a_w_out,
           c_w_in, c_lam, c_g_sub, c_w_out, g_mlp, w_up, w_down, g_final):
    b, s, d = x_prompt.shape
    bs, ts, _ = x_sample.shape
    past = cache_mla_ckv.shape[2]
    n, ns = b * s, bs * ts
    tm = _row_tile(s, 512)
    tq = tk = tm
    tms = _row_tile(ns, 512)
    lam_init = 0.8 - 0.6 * math.exp(-0.3 * 1)
    row = lambda a: a[None, :]

    xp = x_prompt.reshape(n, d)
    xs = x_sample.reshape(ns, d)
    pos_p = jnp.arange(s)
    pos_s = past + jnp.arange(ts)

    win_p, wuq_p, wkv_p, bf_slab = _prep_a(a_w_in[0], a_w_uq[0], a_w_ukv[0], a_b_f[0])
    wa = (row(g_mix[0]), win_p, row(a_g_q[0]), wuq_p, row(a_g_kv[0]), wkv_p, bf_slab)
    tab_ap = _tables_a(pos_p)
    tab_as = jnp.tile(_tables_a(pos_s), (tms // ts, 1))
    (ckv_p, krope_p, fk_p, fv_p, logf_p, lfs_p, qm_p, km_p, vm_p, fq_p, fkb_p, fvb_p) = _proj_a(
        xp, tab_ap, s, *wa, tm, True)
    (ckv_s, krope_s, fk_s, fv_s, logf_s, lfs_s, qm_s, km_s, vm_s, fq_s, fkb_s, fvb_s) = _proj_a(
        xs, tab_as, tms, *wa, tms, False)

    _, fks_p, fqs_p = _cum_aug(lfs_p.reshape(b, s, LANE), jnp.zeros((b, 1, LANE), F32), fkb_p, fq_p, tm,
                               qt=True)
    om_p = _flash("mla", qm_p, km_p, vm_p, [], b, s, tq)
    of_p = _flash("fox", fqs_p, fks_p, fvb_p, [], b, s, tq)

    lf_past = jnp.pad(cache_fox_logf[0], ((0, 0), (0, 0), (0, LANE - H_FOX)))
    cum_past, fks_past = _cum_aug(lf_past, jnp.zeros((bs, 1, LANE), F32),
                                  cache_fox_k[0].reshape(bs * past, -1), None, _row_tile(past, 512))
    rpad = 128
    pad_rows = lambda a: jnp.pad(a.reshape(bs, ts, -1), ((0, 0), (0, rpad - ts), (0, 0)))
    _, fks_new, fqs_new = _cum_aug(pad_rows(lfs_s), cum_past[:, past - 1:past, :],
                                   pad_rows(fkb_s).reshape(bs * rpad, -1),
                                   pad_rows(fq_s).reshape(bs * rpad, -1), rpad)
    kr_past = jnp.pad(cache_mla_krope[0].reshape(bs * past, MLA_ROPE), ((0, 0), (0, LANE - MLA_ROPE)))
    km_past, vm_past = _kv_up(cache_mla_ckv[0].reshape(bs * past, KV_LORA), kr_past, wkv_p,
                              _row_tile(bs * past, 1024))
    om_s = _sample_attn("mla", qm_s, km_past.reshape(bs, past, -1), km_s,
                        vm_past.reshape(bs, past, -1), vm_s, [], bs, ts, past)
    of_s = _sample_attn("fox", fqs_new.reshape(bs, rpad, -1), fks_past.reshape(bs, past, -1),
                        fks_new.reshape(bs, rpad, -1), cache_fox_v[0].reshape(bs, past, -1), fvb_s,
                        [], bs, ts, past)

    wout_a = a_w_out[0].astype(BF16)
    wup0, wdown0 = w_up[0].astype(BF16), w_down[0].astype(BF16)
    x1_p = _outproj_mlp([om_p, of_p], xp, wout_a, row(g_mlp[0]), wup0, wdown0, None, tm)
    x1_s = _outproj_mlp([om_s, of_s], xs, wout_a, row(g_mlp[0]), wup0, wdown0, None, tms)

    wc = c_w_in[0].astype(BF16)
    tab_cp = _tables_c(pos_p)
    tab_cs = jnp.tile(_tables_c(pos_s), (tms // ts, 1))
    dk_p, dv_p, dqb_p, dkb_p, dvb_p = _proj_c(x1_p, tab_cp, s, row(g_mix[1]), wc, tm, True)
    dk_s, dv_s, dqb_s, dkb_s, dvb_s = _proj_c(x1_s, tab_cs, tms, row(g_mix[1]), wc, tms, False)
    extra = [c_lam[0], row(c_g_sub[0])]
    oc_p = _flash("diff", dqb_p, dkb_p, dvb_p, extra, b, s, tq, lam_init)
    oc_s = _sample_attn_heads(dqb_s, cache_diff_k[0].reshape(bs, past * H_DIFF, LANE), dkb_s,
                              cache_diff_v[0].reshape(bs, past * H_DIFF, LANE), dvb_s, extra,
                              bs, ts, past, lam_init)
    wout_c = c_w_out[0].astype(BF16)
    wup1, wdown1 = w_up[1].astype(BF16), w_down[1].astype(BF16)
    y_p = _outproj_mlp([oc_p], x1_p, wout_c, row(g_mlp[1]), wup1, wdown1, row(g_final), tm)
    y_s = _outproj_mlp([oc_s], x1_s, wout_c, row(g_mlp[1]), wup1, wdown1, row(g_final), tms)

    def pack(bb, tt, ckv, krope, fk, fv, logf, dk, dv):
        return (ckv.reshape(1, bb, tt, KV_LORA), krope.reshape(1, bb, tt, MLA_ROPE),
                fk.reshape(1, bb, tt, H_FOX, FOX_DH), fv.reshape(1, bb, tt, H_FOX, FOX_DH),
                logf.reshape(1, bb, tt, H_FOX),
                dk.reshape(1, bb, tt, H_DIFF, 2 * DIFF_DH), dv.reshape(1, bb, tt, H_DIFF, 2 * DIFF_DH))

    return ((y_p.reshape(b, s, d), y_s.reshape(bs, ts, d))
            + pack(b, s, ckv_p, krope_p, fk_p, fv_p, logf_p, dk_p, dv_p)
            + pack(bs, ts, ckv_s, krope_s, fk_s, fv_s, logf_s, dk_s, dv_s))
```

```python
import functools
import math

import jax
import jax.numpy as jnp
from jax import lax
from jax.experimental import pallas as pl
from jax.experimental.pallas import tpu as pltpu

F32 = jnp.float32
BF16 = jnp.bfloat16

CHUNK = 64
ROPE_THETA = 500000.0
H_MLA, MLA_NOPE, MLA_ROPE, MLA_V = 8, 64, 32, 64
Q_LORA, KV_LORA = 512, 256
H_FOX, FOX_DH = 8, 64
H_DIFF, DIFF_DH = 8, 64
DIFF_ROT = DIFF_DH // 4
LANE = 128
LOG2E = math.log2(math.e)
NEG = -1e30
VMEM_LIMIT = 56 * 1024 * 1024


def _cparams(*sem):
    return pltpu.CompilerParams(dimension_semantics=sem, vmem_limit_bytes=VMEM_LIMIT)


def _const_spec(shape):
    nd = len(shape)
    return pl.BlockSpec(shape, lambda *_: (0,) * nd, pipeline_mode=pl.Buffered(1))


def _rms(x, g, eps):
    y = x * lax.rsqrt(jnp.mean(x * x, axis=-1, keepdims=True) + eps)
    return y * g


def _dot(a, b):
    return jnp.dot(a, b, preferred_element_type=F32)


def _rope_slab(x, c, s1, s2, half):
    return x * c + pltpu.roll(x, half, 1) * s1 + pltpu.roll(x, LANE - half, 1) * s2


def _rope_wide(x, tab, base, half):
    c = tab[:, base:base + LANE]
    s1 = tab[:, base + LANE:base + 2 * LANE]
    s2 = tab[:, base + 2 * LANE:base + 3 * LANE]
    n = x.shape[1] // LANE
    return jnp.concatenate(
        [_rope_slab(x[:, j * LANE:(j + 1) * LANE], c, s1, s2, half) for j in range(n)], axis=1)


def _split3(x):
    hi = x.astype(BF16)
    r1 = x - hi.astype(F32)
    mid = r1.astype(BF16)
    lo = (r1 - mid.astype(F32)).astype(BF16)
    return hi, mid, lo


_O_Q, _O_CKV, _O_FQ, _O_FK, _O_FV, _O_KR, _O_F, _A_COLS = 0, 512, 768, 1280, 1792, 2304, 2432, 2560


def _store_v(v_o, v, vt):
    if not vt:
        v_o[...] = v.astype(BF16)
        return
    for p in range(v.shape[1] // LANE):
        v_o[p, 0] = v[:, p * LANE:(p + 1) * LANE].T.astype(BF16)


def _v_out(n, width, tm, vt, row_block=lambda i: i):
    if not vt:
        return (pl.BlockSpec((tm, width), lambda *g: (row_block(*g), 0)),
                jax.ShapeDtypeStruct((n, width), BF16))
    p = width // LANE
    return (pl.BlockSpec((p, 1, LANE, tm), lambda *g: (0, row_block(*g), 0, 0)),
            jax.ShapeDtypeStruct((p, n // tm, LANE, tm), BF16))


def _proj_a_kernel(x_ref, tab_ref, gmix_ref, win_ref, gq_ref, wuq_ref, gkv_ref, wkv_ref, bf_ref,
                   ckv_o, krope_o, fk_o, fv_o, logf_o, lfs_o, qm_o, km_o, vm_o, fq_o, fkb_o, fvb_o,
                   *, fox_scale, vt):
    h = _rms(x_ref[...], gmix_ref[...], 1e-6).astype(BF16)
    tab = tab_ref[...]
    ql = _dot(h, win_ref[:, _O_Q:_O_Q + Q_LORA])
    qn = _rms(ql, gq_ref[...], 1e-6).astype(BF16)
    q = _dot(qn, wuq_ref[...])
    _store_v(qm_o, _rope_wide(q, tab, 0, MLA_ROPE // 2), vt)
    cn = _rms(_dot(h, win_ref[:, _O_CKV:_O_CKV + KV_LORA]), gkv_ref[...], 1e-6)
    ckv_o[...] = cn
    kr = _rope_wide(_dot(h, win_ref[:, _O_KR:_O_KR + LANE]), tab, 3 * LANE, MLA_ROPE // 2)
    krope_o[...] = kr[:, :MLA_ROPE]
    kin = jnp.concatenate([cn.astype(BF16), kr.astype(BF16)], axis=1)
    kv = _dot(kin, wkv_ref[...])
    km_o[...] = kv[:, :H_MLA * LANE].astype(BF16)
    _store_v(vm_o, kv[:, H_MLA * LANE:], vt)
    fq_o[...] = (_dot(h, win_ref[:, _O_FQ:_O_FQ + 512]) * fox_scale).astype(BF16)
    fk = _dot(h, win_ref[:, _O_FK:_O_FK + 512])
    fkb_o[...] = fk.astype(BF16)
    fv = _dot(h, win_ref[:, _O_FV:_O_FV + 512])
    _store_v(fvb_o, fv, vt)
    fk_o[...] = pltpu.einshape("m(hd)->mhd", fk, h=H_FOX)
    fv_o[...] = pltpu.einshape("m(hd)->mhd", fv, h=H_FOX)
    z = _dot(h, win_ref[:, _O_F:_O_F + LANE]) + bf_ref[...]
    ls = jnp.minimum(z, 0.0) - jnp.log1p(jnp.exp(-jnp.abs(z)))
    lane = lax.broadcasted_iota(jnp.int32, ls.shape, 1)
    ls = jnp.where(lane < H_FOX, ls, 0.0)
    lfs_o[...] = ls
    logf_o[...] = ls[:, :H_FOX]


def _proj_a(x, tab, tab_period, gmix, win_p, gq, wuq_p, gkv, wkv_p, bf_slab, tm, vt):
    n, d = x.shape
    nt = n // tm
    nper = tab_period // tm
    row = lambda w: pl.BlockSpec((tm, w), lambda i: (i, 0))
    plain = lambda w, t: (row(w), jax.ShapeDtypeStruct((n, w), t))
    heads = lambda: (pl.BlockSpec((tm, H_FOX, FOX_DH), lambda i: (i, 0, 0)),
                     jax.ShapeDtypeStruct((n, H_FOX, FOX_DH), F32))
    outs = [plain(KV_LORA, F32), plain(MLA_ROPE, F32), heads(), heads(),
            plain(H_FOX, F32), plain(LANE, F32),
            _v_out(n, H_MLA * LANE, tm, vt), plain(H_MLA * LANE, BF16), _v_out(n, H_MLA * MLA_V, tm, vt),
            plain(512, BF16), plain(512, BF16), _v_out(n, 512, tm, vt)]
    return pl.pallas_call(
        functools.partial(_proj_a_kernel, fox_scale=FOX_DH ** -0.5 * LOG2E, vt=vt),
        grid=(nt,),
        in_specs=[row(d),
                  pl.BlockSpec((tm, 6 * LANE), lambda i: (i % nper, 0)),
                  _const_spec(gmix.shape), _const_spec(win_p.shape), _const_spec(gq.shape),
                  _const_spec(wuq_p.shape), _const_spec(gkv.shape), _const_spec(wkv_p.shape),
                  _const_spec(bf_slab.shape)],
        out_specs=[o[0] for o in outs],
        out_shape=[o[1] for o in outs],
        compiler_params=_cparams("parallel"),
        name="proj_a",
    )(x, tab, gmix, win_p, gq, wuq_p, gkv, wkv_p, bf_slab)


def _with_aug(x, aug):
    x = x.astype(F32)
    first = lax.broadcasted_iota(jnp.int32, (x.shape[0], LANE), 1) < LANE // 2
    out = []
    for p in range(H_FOX // 2):
        pair = x[:, p * LANE:(p + 1) * LANE]
        out.append(jnp.where(first, pair, aug[:, 2 * p * LANE:(2 * p + 1) * LANE]))
        out.append(jnp.where(first, pltpu.roll(pair, LANE // 2, 1),
                             aug[:, (2 * p + 1) * LANE:(2 * p + 2) * LANE]))
    return jnp.concatenate(out, axis=1)


def _cum_kernel(*refs, with_q, with_v, qt):
    lf_ref, init_ref, tri_ref, ek_ref, eq_ref, onek_ref, oneq_ref, fk_ref = refs[:8]
    fq_ref = refs[8] if with_q else None
    fv_ref = refs[8 + with_q] if with_v else None
    outs = refs[8 + with_q + with_v:-1]
    carry = refs[-1]
    rows_of = lambda r: pltpu.einshape("mhd->m(hd)", r[...]) if len(r.shape) == 3 else r[...]

    @pl.when(pl.program_id(1) == 0)
    def _():
        carry[...] = init_ref[0]

    pieces = lambda x: jnp.concatenate(_split3(x), axis=1)
    c3 = _dot(tri_ref[...], pieces(lf_ref[0]))
    cum = carry[...] + (c3[:, :LANE] + c3[:, LANE:2 * LANE] + c3[:, 2 * LANE:])
    t = cum.shape[0]
    carry[...] = cum[t - 1:t, :]
    outs[0][0] = cum
    p3 = pieces(cum * LOG2E)
    kaug = _dot(p3, ek_ref[...]) + onek_ref[...]
    outs[1][...] = _with_aug(rows_of(fk_ref), kaug).astype(BF16)
    if with_q:
        qaug = _dot(p3, eq_ref[...]) + oneq_ref[...]
        _store_v(outs[2], _with_aug(fq_ref[...], qaug), qt)
    if with_v:
        outs[2 + with_q][...] = rows_of(fv_ref).astype(BF16)


def _aug_tables():
    w = H_FOX * LANE
    ek = [[[0.0] * w for _ in range(LANE)] for _ in range(3)]
    eq = [[[0.0] * w for _ in range(LANE)] for _ in range(3)]
    onek = [0.0] * w
    oneq = [0.0] * w
    for hh in range(H_FOX):
        base = hh * LANE + FOX_DH
        for s in range(3):
            ek[s][hh][base + s] = -1.0
            eq[s][hh][base + 3 + s] = 1.0
            onek[base + 3 + s] = 1.0
            oneq[base + s] = 1.0
    return (jnp.array(ek, BF16).reshape(3 * LANE, w), jnp.array(eq, BF16).reshape(3 * LANE, w),
            jnp.array([onek], F32), jnp.array([oneq], F32))


def _cum_aug(lf_slab, init, fk, fq, t, qt=False, fv=None):
    b, l, _ = lf_slab.shape
    nt = l // t
    tri = (lax.broadcasted_iota(jnp.int32, (t, t), 1)
           <= lax.broadcasted_iota(jnp.int32, (t, t), 0)).astype(BF16)
    consts = (tri,) + _aug_tables()
    w = H_FOX * LANE
    rows = lambda width: pl.BlockSpec((t, width), lambda i, j: (i * nt + j, 0))
    in_rows = lambda x: (rows(x.shape[1]) if x.ndim == 2 else
                         pl.BlockSpec((t,) + x.shape[1:], lambda i, j: (i * nt + j, 0, 0)))
    xs = [fk] + ([fq] if fq is not None else []) + ([fv] if fv is not None else [])
    outs = [(pl.BlockSpec((1, t, LANE), lambda i, j: (i, j, 0)), jax.ShapeDtypeStruct((b, l, LANE), F32)),
            (rows(w), jax.ShapeDtypeStruct((b * l, w), BF16))]
    if fq is not None:
        outs.append(_v_out(b * l, w, t, qt, lambda i, j: i * nt + j))
    if fv is not None:
        outs.append((rows(H_FOX * FOX_DH), jax.ShapeDtypeStruct((b * l, H_FOX * FOX_DH), BF16)))
    return pl.pallas_call(
        functools.partial(_cum_kernel, with_q=fq is not None, with_v=fv is not None, qt=qt),
        grid=(b, nt),
        in_specs=([pl.BlockSpec((1, t, LANE), lambda i, j: (i, j, 0)),
                   pl.BlockSpec((1, 1, LANE), lambda i, j: (i, 0, 0))]
                  + [_const_spec(c.shape) for c in consts] + [in_rows(x) for x in xs]),
        out_specs=[o[0] for o in outs],
        out_shape=[o[1] for o in outs],
        scratch_shapes=[pltpu.VMEM((1, LANE), F32)],
        compiler_params=_cparams("arbitrary", "arbitrary"),
        name="cum_aug",
    )(lf_slab, init, *consts, *xs)


def _softmax_step(qs, ks, v, mask, m_ref, l_ref, acc_ref):
    tk = v.shape[0]
    for j in range(2):
        s = lax.dot_general(qs[j], ks[j], (((1,), (1,)), ((), ())), preferred_element_type=F32)
        if mask is not None:
            s = jnp.where(mask, s, NEG)
        m_prev = m_ref[j]
        m_next = jnp.maximum(m_prev, jnp.max(s, axis=1, keepdims=True))
        p = jnp.exp2(s - jnp.concatenate([m_next] * (tk // LANE), axis=1)) if tk >= LANE \
            else jnp.exp2(s - m_next[:, :tk])
        alpha = jnp.exp2(m_prev - m_next)
        l_ref[j] = alpha * l_ref[j] + jnp.sum(p, axis=1, keepdims=True)
        m_ref[j] = m_next
        acc_ref[j] = acc_ref[j] * alpha + _dot(p.astype(BF16), v)


def _init_stats(m_ref, l_ref, acc_ref):
    m_ref[...] = jnp.full(m_ref.shape, NEG, F32)
    l_ref[...] = jnp.zeros(l_ref.shape, F32)
    acc_ref[...] = jnp.zeros(acc_ref.shape, F32)


def _half_masks(shape):
    lane = lax.broadcasted_iota(jnp.int32, shape, 1)
    return (lane % LANE) < (LANE // 2)


def _queries(q):
    return [q[:, :LANE], q[:, LANE:]]


def _keys(mode, k):
    return [k, k] if mode == "diff" else [k[:, :LANE], k[:, LANE:]]


def _visible(mode, q_pos, k_pos):
    if mode == "fox":
        return k_pos <= q_pos
    return (k_pos // CHUNK) <= (q_pos // CHUNK)


def _finish(mode, l_ref, acc_ref, extra, lam_init):
    o0 = acc_ref[0] / l_ref[0]
    o1 = acc_ref[1] / l_ref[1]
    if mode != "diff":
        return jnp.where(_half_masks(o0.shape), o0, o1)
    lam_ref, gsub_ref = extra
    lp = lam_ref[...]
    lam = (jnp.exp(jnp.sum(lp[0:1] * lp[1:2], axis=1, keepdims=True))
           - jnp.exp(jnp.sum(lp[2:3] * lp[3:4], axis=1, keepdims=True)) + lam_init)
    o = o0 - lam * o1
    return _rms(o, gsub_ref[...], 1e-5) * (1.0 - lam_init)


ONES_ROWS = 16


def _softmax_step_t(st, mx, vt1, mask, m_ref, acc_ref):
    if mask is not None:
        st = jnp.where(mask, st, NEG)
        mx = jnp.max(st, axis=0, keepdims=True)
    m_prev = m_ref[...]
    m_next = jnp.maximum(m_prev, mx)
    pt = jnp.exp2(st - m_next)
    alpha = jnp.exp2(m_prev - m_next)
    m_ref[...] = m_next
    acc_ref[...] = acc_ref[...] * alpha + _dot(vt1, pt.astype(BF16))


def _finish_t(mode, acc_ref, extra, lam_init):
    rows = acc_ref.shape[1] - ONES_ROWS
    o0 = acc_ref[0, :rows] / acc_ref[0, rows:rows + 1]
    o1 = acc_ref[1, :rows] / acc_ref[1, rows:rows + 1]
    if mode != "diff":
        return jnp.concatenate([o0, o1], axis=0).T
    lam_ref, gsub_ref = extra
    lp = lam_ref[...]
    lam = (jnp.exp(jnp.sum(lp[0:1] * lp[1:2], axis=1, keepdims=True))
           - jnp.exp(jnp.sum(lp[2:3] * lp[3:4], axis=1, keepdims=True)) + lam_init)
    o = (o0 - lam * o1).T
    return _rms(o, gsub_ref[...], 1e-5) * (1.0 - lam_init)


def _flash_kernel(qt_ref, k_ref, vt_ref, *rest, mode, nq, t, lam_init):
    n_extra = 2 if mode == "diff" else 0
    extra = rest[:n_extra]
    o_ref = rest[n_extra]
    m_ref, acc_ref, st_ref, mx_ref = rest[n_extra + 1:]
    half = LANE // 2
    rows_of = lambda blk: pl.ds(pl.multiple_of(blk * t, t), t)

    def produce(task, slot, j):
        qi, kb = task
        k = _keys(mode, k_ref[rows_of(kb), :])[j]
        st = _dot(k, qt_ref[j, qi])
        st_ref[slot, j] = st
        mx_ref[slot, j] = jnp.max(st, axis=0, keepdims=True)

    def fold(kb, slot, j, diagonal):
        vt = vt_ref[kb]
        if mode != "diff":
            vt = vt[j * half:(j + 1) * half]
        vt1 = jnp.concatenate([vt, jnp.ones((ONES_ROWS, t), BF16)], axis=0)
        msk = _visible(mode, lax.broadcasted_iota(jnp.int32, (t, t), 1),
                       lax.broadcasted_iota(jnp.int32, (t, t), 0)) if diagonal else None
        _softmax_step_t(st_ref[slot, j], mx_ref[slot, j], vt1, msk, m_ref.at[j], acc_ref.at[j])

    def both(cur, slot, nxt, diagonal):
        for j in range(2):
            produce(nxt, 1 - slot, j)
            fold(cur[1], slot, j, diagonal)

    def stage(cur, slot, nxt):
        qi, kb = cur

        @pl.when(kb == 0)
        def _():
            m_ref[...] = jnp.full(m_ref.shape, NEG, F32)
            acc_ref[...] = jnp.zeros(acc_ref.shape, F32)

        @pl.when(kb != qi)
        def _():
            both(cur, slot, nxt, False)

        @pl.when(kb == qi)
        def _():
            both(cur, slot, nxt, True)
            o_ref[rows_of(qi), :] = _finish_t(mode, acc_ref, extra, lam_init).astype(o_ref.dtype)

    def succ(task):
        qi, kb = task
        diag = kb == qi
        nqi = jnp.where(diag, qi + 1, qi)
        over = nqi >= nq
        return jnp.where(over, 0, nqi), jnp.where(diag | over, 0, kb + 1)

    first = (jnp.int32(0), jnp.int32(0))
    for j in range(2):
        produce(first, 0, j)

    def pair(_, task):
        nxt = succ(task)
        nxt2 = succ(nxt)
        stage(task, 0, nxt)
        stage(nxt, 1, nxt2)
        return nxt2

    n_tasks = nq * (nq + 1) // 2
    last = lax.fori_loop(0, n_tasks // 2, pair, first)
    if n_tasks % 2:
        stage(last, 0, first)


def _flash(mode, qt, k, vt, extra, b, s, t, lam_init=0.0):
    p = vt.shape[0]
    wk = k.shape[1] // p
    rows = LANE if mode == "diff" else LANE // 2
    seq = lambda w: pl.BlockSpec((s, w), lambda bi, pi: (bi, pi))
    return pl.pallas_call(
        functools.partial(_flash_kernel, mode=mode, nq=s // t, t=t, lam_init=lam_init),
        grid=(b, p),
        in_specs=([pl.BlockSpec((2, s // t, LANE, t), lambda bi, pi: (pi, bi, 0, 0)), seq(wk),
                   pl.BlockSpec((None, s // t, LANE, t), lambda bi, pi: (pi, bi, 0, 0))]
                  + [_const_spec(e.shape) for e in extra]),
        out_specs=seq(LANE),
        out_shape=jax.ShapeDtypeStruct((b * s, p * LANE), BF16),
        scratch_shapes=[pltpu.VMEM((2, 1, t), F32),
                        pltpu.VMEM((2, rows + ONES_ROWS, t), F32),
                        pltpu.VMEM((2, 2, t, t), F32), pltpu.VMEM((2, 2, 1, t), F32)],
        compiler_params=_cparams("parallel", "parallel"),
        name="flash_" + mode,
    )(qt, k, vt, *extra)


def _sample_kernel(q_ref, kp_ref, kn_ref, vp_ref, vn_ref, *rest, mode, past, tq, tkp, lam_init):
    n_extra = 2 if mode == "diff" else 0
    extra = rest[:n_extra]
    o_ref = rest[n_extra]
    m_ref, l_ref, acc_ref = rest[n_extra + 1:]
    _init_stats(m_ref, l_ref, acc_ref)
    new = lambda r: (r[0] if len(r.shape) == 3 else r[...])[:tq]
    qs = _queries(new(q_ref))
    for c in range(past // tkp):
        sl = slice(c * tkp, (c + 1) * tkp)
        ks = _keys(mode, kp_ref[0, sl, :].astype(BF16))
        _softmax_step(qs, ks, vp_ref[0, sl, :].astype(BF16), None, m_ref, l_ref, acc_ref)
    pad = lambda a: jnp.concatenate([a, jnp.zeros((LANE - tq, a.shape[1]), a.dtype)], axis=0)
    q_pos = past + lax.broadcasted_iota(jnp.int32, (tq, LANE), 0)
    k_idx = lax.broadcasted_iota(jnp.int32, (tq, LANE), 1)
    mask = _visible(mode, q_pos, past + k_idx) & (k_idx < tq)
    _softmax_step(qs, _keys(mode, pad(new(kn_ref))), pad(vn_ref[...]), mask, m_ref, l_ref, acc_ref)
    o_ref[...] = _finish(mode, l_ref, acc_ref, extra, lam_init).astype(o_ref.dtype)


def _sample_attn(mode, q, kp, kn, vp, vn, extra, bs, tq, past, lam_init=0.0):
    p = vn.shape[1] // LANE
    tkp = min(past, 1024)

    def new_spec(a):
        w = a.shape[-1] // p
        if a.ndim == 3:
            return pl.BlockSpec((1, a.shape[1], w), lambda bi, pi: (bi, 0, pi))
        return pl.BlockSpec((tq, w), lambda bi, pi: (bi, pi))

    def past_spec(a):
        return pl.BlockSpec((1, past, a.shape[-1] // p), lambda bi, pi: (bi, 0, pi))

    return pl.pallas_call(
        functools.partial(_sample_kernel, mode=mode, past=past, tq=tq, tkp=tkp, lam_init=lam_init),
        grid=(bs, p),
        in_specs=([new_spec(q), past_spec(kp), new_spec(kn), past_spec(vp), new_spec(vn)]
                  + [_const_spec(e.shape) for e in extra]),
        out_specs=pl.BlockSpec((tq, LANE), lambda bi, pi: (bi, pi)),
        out_shape=jax.ShapeDtypeStruct((bs * tq, p * LANE), BF16),
        scratch_shapes=[pltpu.VMEM((2, tq, LANE), F32)] * 3,
        compiler_params=_cparams("parallel", "parallel"),
        name="sample_" + mode,
    )(q, kp, kn, vp, vn, *extra)


def _sample_heads_kernel(q_ref, kp_ref, kn_ref, vp_ref, vn_ref, lam_ref, gsub_ref, o_ref,
                         m_ref, l_ref, acc_ref, *, past, tq, tkp, lam_init):
    c = pl.program_id(1)

    @pl.when(c == 0)
    def _():
        _init_stats(m_ref, l_ref, acc_ref)

    def stats(hh):
        return [r.at[2 * hh:2 * hh + 2] for r in (m_ref, l_ref, acc_ref)]

    def queries(hh):
        return _queries(q_ref[:, 2 * hh * LANE:(2 * hh + 2) * LANE])

    for hh in range(H_DIFF):
        head_rows = pl.ds(hh, tkp, stride=H_DIFF)
        k = kp_ref[0, head_rows, :].astype(BF16)
        v = vp_ref[0, head_rows, :].astype(BF16)
        _softmax_step(queries(hh), [k, k], v, None, *stats(hh))

    @pl.when(c == pl.num_programs(1) - 1)
    def _():
        pad = lambda a: jnp.concatenate([a, jnp.zeros((LANE - tq, a.shape[1]), a.dtype)], axis=0)
        q_pos = past + lax.broadcasted_iota(jnp.int32, (tq, LANE), 0)
        k_idx = lax.broadcasted_iota(jnp.int32, (tq, LANE), 1)
        mask = _visible("diff", q_pos, past + k_idx) & (k_idx < tq)
        for hh in range(H_DIFF):
            sl = slice(hh * LANE, (hh + 1) * LANE)
            k = pad(kn_ref[:, sl])
            _softmax_step(queries(hh), [k, k], pad(vn_ref[:, sl]), mask, *stats(hh))
            m_h, l_h, acc_h = stats(hh)
            o_ref[:, sl] = _finish("diff", l_h, acc_h, (lam_ref, gsub_ref), lam_init).astype(o_ref.dtype)


def _sample_attn_heads(q, kp, kn, vp, vn, extra, bs, tq, past, lam_init):
    tkp = min(past, 1024)
    new = lambda a: pl.BlockSpec((tq, a.shape[1]), lambda bi, ci: (bi, 0))
    chunk = pl.BlockSpec((1, tkp * H_DIFF, LANE), lambda bi, ci: (bi, ci, 0))
    return pl.pallas_call(
        functools.partial(_sample_heads_kernel, past=past, tq=tq, tkp=tkp, lam_init=lam_init),
        grid=(bs, past // tkp),
        in_specs=[new(q), chunk, new(kn), chunk, new(vn)] + [_const_spec(e.shape) for e in extra],
        out_specs=pl.BlockSpec((tq, H_DIFF * LANE), lambda bi, ci: (bi, 0)),
        out_shape=jax.ShapeDtypeStruct((bs * tq, H_DIFF * LANE), BF16),
        scratch_shapes=[pltpu.VMEM((2 * H_DIFF, tq, LANE), F32)] * 3,
        compiler_params=_cparams("parallel", "arbitrary"),
        name="sample_diff",
    )(q, kp, kn, vp, vn, *extra)


def _kv_up_kernel(ckv_ref, kr_ref, wkv_ref, k_o, v_o):
    kin = jnp.concatenate([ckv_ref[...].astype(BF16), kr_ref[...].astype(BF16)], axis=1)
    kv = _dot(kin, wkv_ref[...])
    k_o[...] = kv[:, :H_MLA * LANE].astype(BF16)
    v_o[...] = kv[:, H_MLA * LANE:].astype(BF16)


def _kv_up(ckv, kr_slab, wkv_p, tm):
    n = ckv.shape[0]
    return pl.pallas_call(
        _kv_up_kernel,
        grid=(n // tm,),
        in_specs=[pl.BlockSpec((tm, KV_LORA), lambda i: (i, 0)),
                  pl.BlockSpec((tm, LANE), lambda i: (i, 0)),
                  _const_spec(wkv_p.shape)],
        out_specs=[pl.BlockSpec((tm, H_MLA * LANE), lambda i: (i, 0)),
                   pl.BlockSpec((tm, H_MLA * MLA_V), lambda i: (i, 0))],
        out_shape=[jax.ShapeDtypeStruct((n, H_MLA * LANE), BF16),
                   jax.ShapeDtypeStruct((n, H_MLA * MLA_V), BF16)],
        compiler_params=_cparams("parallel"),
        name="kv_up",
    )(ckv, kr_slab, wkv_p)


def _mlp_kernel(*refs, n_o, final, ff_chunk):
    o_refs = refs[:n_o]
    x_ref, wout_ref, gmlp_ref, wup_ref, wdown_ref = refs[n_o:n_o + 5]
    gfin_ref = refs[n_o + 5] if final else None
    out_ref = refs[-1]
    proj = None
    r0 = 0
    for o_ref in o_refs:
        w = o_ref.shape[1]
        t = _dot(o_ref[...], wout_ref[r0:r0 + w, :])
        proj = t if proj is None else proj + t
        r0 += w
    x1 = x_ref[...] + proj
    h = _rms(x1, gmlp_ref[...], 1e-6).astype(BF16)
    down = None
    for c in range(wup_ref.shape[1] // ff_chunk):
        u = _dot(h, wup_ref[:, c * ff_chunk:(c + 1) * ff_chunk])
        a = jnp.square(jnp.maximum(u, 0.0)).astype(BF16)
        t = _dot(a, wdown_ref[c * ff_chunk:(c + 1) * ff_chunk, :])
        down = t if down is None else down + t
    x2 = x1 + down
    out_ref[...] = _rms(x2, gfin_ref[...], 1e-6) if final else x2


def _outproj_mlp(os_, x, wout, gmlp, wup, wdown, gfin, tm):
    n, d = x.shape
    final = gfin is not None
    consts = [wout, gmlp, wup, wdown] + ([gfin] if final else [])
    return pl.pallas_call(
        functools.partial(_mlp_kernel, n_o=len(os_), final=final, ff_chunk=1024),
        grid=(n // tm,),
        in_specs=([pl.BlockSpec((tm, o.shape[1]), lambda i: (i, 0)) for o in os_]
                  + [pl.BlockSpec((tm, d), lambda i: (i, 0))]
                  + [_const_spec(c.shape) for c in consts]),
        out_specs=pl.BlockSpec((tm, d), lambda i: (i, 0)),
        out_shape=jax.ShapeDtypeStruct((n, d), F32),
        compiler_params=_cparams("parallel"),
        name="outproj_mlp",
    )(*os_, x, *consts)


def _proj_c_kernel(x_ref, tab_ref, gmix_ref, w_ref, k_o, v_o, qb_o, kb_o, vb_o, *, vt):
    h = _rms(x_ref[...], gmix_ref[...], 1e-6).astype(BF16)
    tab = tab_ref[...]
    cw = H_DIFF * 2 * DIFF_DH
    q = _rope_wide(_dot(h, w_ref[:, :cw]), tab, 0, DIFF_ROT // 2)
    first = lax.broadcasted_iota(jnp.int32, (q.shape[0], LANE), 1) < DIFF_DH
    slabs = []
    for hh in range(H_DIFF):
        qh = q[:, hh * LANE:(hh + 1) * LANE]
        slabs += [jnp.where(first, qh, 0.0), jnp.where(first, 0.0, qh)]
    _store_v(qb_o, jnp.concatenate(slabs, axis=1), vt)
    k = _rope_wide(_dot(h, w_ref[:, cw:2 * cw]), tab, 3 * LANE, DIFF_ROT // 2)
    kb_o[...] = k.astype(BF16)
    v = _dot(h, w_ref[:, 2 * cw:])
    _store_v(vb_o, v, vt)
    k_o[...] = pltpu.einshape("m(hd)->mhd", k, h=H_DIFF)
    v_o[...] = pltpu.einshape("m(hd)->mhd", v, h=H_DIFF)


def _proj_c(x, tab, tab_period, gmix, w, tm, vt):
    n, d = x.shape
    nper = tab_period // tm
    cw = H_DIFF * 2 * DIFF_DH
    plain = lambda t, w=cw: (pl.BlockSpec((tm, w), lambda i: (i, 0)), jax.ShapeDtypeStruct((n, w), t))
    heads = lambda: (pl.BlockSpec((tm, H_DIFF, LANE), lambda i: (i, 0, 0)),
                     jax.ShapeDtypeStruct((n, H_DIFF, LANE), F32))
    outs = [heads(), heads(), _v_out(n, 2 * cw, tm, vt), plain(BF16), _v_out(n, cw, tm, vt)]
    return pl.pallas_call(
        functools.partial(_proj_c_kernel, vt=vt),
        grid=(n // tm,),
        in_specs=[pl.BlockSpec((tm, d), lambda i: (i, 0)),
                  pl.BlockSpec((tm, 6 * LANE), lambda i: (i % nper, 0)),
                  _const_spec(gmix.shape), _const_spec(w.shape)],
        out_specs=[o[0] for o in outs],
        out_shape=[o[1] for o in outs],
        compiler_params=_cparams("parallel"),
        name="proj_c",
    )(x, tab, gmix, w)


def _angles(pos, half):
    inv = ROPE_THETA ** (-jnp.arange(half, dtype=F32) / half)
    ang = pos.astype(F32)[:, None] * inv[None, :]
    return jnp.cos(ang), jnp.sin(ang)


def _tables_a(pos):
    cos, sin = _angles(pos, MLA_ROPE // 2)
    t = pos.shape[0]
    z = lambda w: jnp.zeros((t, w), F32)
    sc = (MLA_NOPE + MLA_ROPE) ** -0.5 * LOG2E
    cq = sc * jnp.concatenate([jnp.ones((t, MLA_NOPE), F32), cos, cos, z(32)], axis=1)
    s1q = sc * jnp.concatenate([z(MLA_NOPE + 16), sin, z(32)], axis=1)
    s2q = sc * jnp.concatenate([z(MLA_NOPE), -sin, z(48)], axis=1)
    ck = jnp.concatenate([cos, cos, z(96)], axis=1)
    s1k = jnp.concatenate([z(16), sin, z(96)], axis=1)
    s2k = jnp.concatenate([-sin, z(112)], axis=1)
    return jnp.concatenate([cq, s1q, s2q, ck, s1k, s2k], axis=1)


def _tables_c(pos):
    cos, sin = _angles(pos, DIFF_ROT // 2)
    t = pos.shape[0]
    z = lambda w: jnp.zeros((t, w), F32)
    two = lambda a: jnp.concatenate([a, a], axis=1)
    c = two(jnp.concatenate([cos, cos, jnp.ones((t, 48), F32)], axis=1))
    s1 = two(jnp.concatenate([z(8), sin, z(48)], axis=1))
    s2 = two(jnp.concatenate([-sin, z(56)], axis=1))
    sc = DIFF_DH ** -0.5 * LOG2E
    return jnp.concatenate([sc * c, sc * s1, sc * s2, c, s1, s2], axis=1)


def _pad_cols(a, w):
    return jnp.pad(a, ((0, 0), (0, w - a.shape[1])))


def _prep_a(w_in, w_uq, w_ukv, b_f):
    o1, o2, o3 = Q_LORA, Q_LORA + KV_LORA, Q_LORA + KV_LORA + MLA_ROPE
    o4, o5, o6 = o3 + 512, o3 + 1024, o3 + 1536
    win_p = jnp.concatenate(
        [w_in[:, :o2], w_in[:, o3:o6], _pad_cols(w_in[:, o2:o3], LANE), _pad_cols(w_in[:, o6:], LANE)],
        axis=1).astype(BF16)
    dq = MLA_NOPE + MLA_ROPE
    wuq_p = jnp.pad(w_uq.reshape(Q_LORA, H_MLA, dq), ((0, 0), (0, 0), (0, LANE - dq)))
    wuq_p = wuq_p.reshape(Q_LORA, H_MLA * LANE).astype(BF16)
    wkv3 = w_ukv.reshape(KV_LORA, H_MLA, MLA_NOPE + MLA_V)
    wk = jnp.pad(wkv3[:, :, :MLA_NOPE], ((0, 0), (0, 0), (0, LANE - MLA_NOPE))).reshape(KV_LORA, H_MLA * LANE)
    wv = wkv3[:, :, MLA_NOPE:].reshape(KV_LORA, H_MLA * MLA_V)
    place = jnp.pad(jnp.eye(MLA_ROPE, dtype=F32), ((0, LANE - MLA_ROPE), (MLA_NOPE, LANE - MLA_NOPE - MLA_ROPE)))
    place = jnp.concatenate([jnp.tile(place, (1, H_MLA)), jnp.zeros((LANE, H_MLA * MLA_V), F32)], axis=1)
    wkv_p = jnp.concatenate([jnp.concatenate([wk, wv], axis=1), place], axis=0).astype(BF16)
    bf_slab = _pad_cols(b_f[None, :], LANE)
    return win_p, wuq_p, wkv_p, bf_slab


def _row_tile(n, want):
    t = min(n, want)
    while n % t:
        t //= 2
    return t


def kernel(x_prompt, x_sample, cache_mla_ckv, cache_mla_krope, cache_fox_k, cache_fox_v, cache_fox_logf,
           cache_diff_k, cache_diff_v, g_mix, a_w_in, a_g_q, a_w_uq, a_g_kv, a_w_ukv, a_b_f, a_w_out,
           c_w_in, c_lam, c_g_sub, c_w_out, g_mlp, w_up, w_down, g_final):
    b, s, d = x_prompt.shape
    bs, ts, _ = x_sample.shape
    past = cache_mla_ckv.shape[2]
    n, ns = b * s, bs * ts
    tm = _row_tile(s, 512)
    tq = tk = tm
    tms = _row_tile(ns, 512)
    lam_init = 0.8 - 0.6 * math.exp(-0.3 * 1)
    row = lambda a: a[None, :]

    xp = x_prompt.reshape(n, d)
    xs = x_sample.reshape(ns, d)
    pos_p = jnp.arange(s)
    pos_s = past + jnp.arange(ts)

    win_p, wuq_p, wkv_p, bf_slab = _prep_a(a_w_in[0], a_w_uq[0], a_w_ukv[0], a_b_f[0])
    wa = (row(g_mix[0]), win_p, row(a_g_q[0]), wuq_p, row(a_g_kv[0]), wkv_p, bf_slab)
    tab_ap = _tables_a(pos_p)
    tab_as = jnp.tile(_tables_a(pos_s), (tms // ts, 1))
    (ckv_p, krope_p, fk_p, fv_p, logf_p, lfs_p, qm_p, km_p, vm_p, fq_p, fkb_p, fvb_p) = _proj_a(
        xp, tab_ap, s, *wa, tm, True)
    (ckv_s, krope_s, fk_s, fv_s, logf_s, lfs_s, qm_s, km_s, vm_s, fq_s, fkb_s, fvb_s) = _proj_a(
        xs, tab_as, tms, *wa, tms, False)

    _, fks_p, fqs_p = _cum_aug(lfs_p.reshape(b, s, LANE), jnp.zeros((b, 1, LANE), F32), fkb_p, fq_p, tm,
                               qt=True)
    om_p = _flash("mla", qm_p, km_p, vm_p, [], b, s, tq)
    of_p = _flash("fox", fqs_p, fks_p, fvb_p, [], b, s, tq)

    lf_past = jnp.pad(cache_fox_logf[0], ((0, 0), (0, 0), (0, LANE - H_FOX)))
    cum_past, fks_past, fv_past = _cum_aug(
        lf_past, jnp.zeros((bs, 1, LANE), F32), cache_fox_k[0].reshape(bs * past, H_FOX, FOX_DH), None,
        _row_tile(past, 512), fv=cache_fox_v[0].reshape(bs * past, H_FOX, FOX_DH))
    rpad = 128
    pad_rows = lambda a: jnp.pad(a.reshape(bs, ts, -1), ((0, 0), (0, rpad - ts), (0, 0)))
    _, fks_new, fqs_new = _cum_aug(pad_rows(lfs_s), cum_past[:, past - 1:past, :],
                                   pad_rows(fkb_s).reshape(bs * rpad, -1),
                                   pad_rows(fq_s).reshape(bs * rpad, -1), rpad)
    kr_past = jnp.pad(cache_mla_krope[0].reshape(bs * past, MLA_ROPE), ((0, 0), (0, LANE - MLA_ROPE)))
    km_past, vm_past = _kv_up(cache_mla_ckv[0].reshape(bs * past, KV_LORA), kr_past, wkv_p,
                              _row_tile(bs * past, 1024))
    om_s = _sample_attn("mla", qm_s, km_past.reshape(bs, past, -1), km_s,
                        vm_past.reshape(bs, past, -1), vm_s, [], bs, ts, past)
    of_s = _sample_attn("fox", fqs_new.reshape(bs, rpad, -1), fks_past.reshape(bs, past, -1),
                        fks_new.reshape(bs, rpad, -1), fv_past.reshape(bs, past, -1), fvb_s,
                        [], bs, ts, past)

    wout_a = a_w_out[0].astype(BF16)
    wup0, wdown0 = w_up[0].astype(BF16), w_down[0].astype(BF16)
    x1_p = _outproj_mlp([om_p, of_p], xp, wout_a, row(g_mlp[0]), wup0, wdown0, None, tm)
    x1_s = _outproj_mlp([om_s, of_s], xs, wout_a, row(g_mlp[0]), wup0, wdown0, None, tms)

    wc = c_w_in[0].astype(BF16)
    tab_cp = _tables_c(pos_p)
    tab_cs = jnp.tile(_tables_c(pos_s), (tms // ts, 1))
    dk_p, dv_p, dqb_p, dkb_p, dvb_p = _proj_c(x1_p, tab_cp, s, row(g_mix[1]), wc, tm, True)
    dk_s, dv_s, dqb_s, dkb_s, dvb_s = _proj_c(x1_s, tab_cs, tms, row(g_mix[1]), wc, tms, False)
    extra = [c_lam[0], row(c_g_sub[0])]
    oc_p = _flash("diff", dqb_p, dkb_p, dvb_p, extra, b, s, tq, lam_init)
    oc_s = _sample_attn_heads(dqb_s, cache_diff_k[0].reshape(bs, past * H_DIFF, LANE), dkb_s,
                              cache_diff_v[0].reshape(bs, past * H_DIFF, LANE), dvb_s, extra,
                              bs, ts, past, lam_init)
    wout_c = c_w_out[0].astype(BF16)
    wup1, wdown1 = w_up[1].astype(BF16), w_down[1].astype(BF16)
    y_p = _outproj_mlp([oc_p], x1_p, wout_c, row(g_mlp[1]), wup1, wdown1, row(g_final), tm)
    y_s = _outproj_mlp([oc_s], x1_s, wout_c, row(g_mlp[1]), wup1, wdown1, row(g_final), tms)

    def pack(bb, tt, ckv, krope, fk, fv, logf, dk, dv):
        return (ckv.reshape(1, bb, tt, KV_LORA), krope.reshape(1, bb, tt, MLA_ROPE),
                fk.reshape(1, bb, tt, H_FOX, FOX_DH), fv.reshape(1, bb, tt, H_FOX, FOX_DH),
                logf.reshape(1, bb, tt, H_FOX),
                dk.reshape(1, bb, tt, H_DIFF, 2 * DIFF_DH), dv.reshape(1, bb, tt, H_DIFF, 2 * DIFF_DH))

    return ((y_p.reshape(b, s, d), y_s.reshape(bs, ts, d))
            + pack(b, s, ckv_p, krope_p, fk_p, fv_p, logf_p, dk_p, dv_p)
            + pack(bs, ts, ckv_s, krope_s, fk_s, fv_s, logf_s, dk_s, dv_s))
```
